```python
import jax, jax.numpy as jnp
from jax import lax
import numpy as np

D_MODEL = 1024
BATCH = 4
SEQ = 8192
DEPTH = 2
DEC_BATCH = 32
DEC_SEQ = 64
PAST_LEN = 2048

CHUNK = 64
N_META = 16
N_CONV_LAYERS = (DEPTH + 1) // 2
N_ATTN_LAYERS = DEPTH // 2
CONV_WIDTH = 31
N_HEADS = 16
N_KV_HEADS = 4
GROUP = N_HEADS // N_KV_HEADS
HEAD_DIM = D_MODEL // N_HEADS
N_IDX_HEADS = 8
IDX_DIM = 64
TOPK_MAX = 256
Q_BLOCK = 128
D_FF = 2816
N_EXPERTS = 8
TOP_K_EXPERTS = 2
RMS_EPS = 1e-6
LN_EPS = 1e-5
Q_WIDTH = N_HEADS * HEAD_DIM
KV_WIDTH = N_KV_HEADS * HEAD_DIM
SPLITS = (Q_WIDTH, Q_WIDTH + KV_WIDTH, Q_WIDTH + 2 * KV_WIDTH,
          Q_WIDTH + 2 * KV_WIDTH + N_IDX_HEADS * IDX_DIM,
          Q_WIDTH + 2 * KV_WIDTH + N_IDX_HEADS * IDX_DIM + IDX_DIM)
P_IN = SPLITS[-1] + N_IDX_HEADS

kernel_name = 'hybrid_conformer_dsa_streaming_step'


def rmsnorm(x, g):
    xf = x.astype(jnp.float32)
    y = xf * lax.rsqrt(jnp.mean(xf * xf, axis=-1, keepdims=True) + RMS_EPS)
    return (y * g.astype(jnp.float32)).astype(x.dtype)


def layernorm(x, g, b):
    xf = x.astype(jnp.float32)
    mu = jnp.mean(xf, axis=-1, keepdims=True)
    xc = xf - mu
    var = jnp.mean(xc * xc, axis=-1, keepdims=True)
    return (xc * lax.rsqrt(var + LN_EPS) * g.astype(jnp.float32) + b.astype(jnp.float32)).astype(x.dtype)


def swiglu(h, wg, wu, wd):
    return (jax.nn.silu(h @ wg) * (h @ wu)) @ wd


def conv_module(h, state, w_pw1, b_pw1, dw_w, dw_b, ln_g, ln_b, w_pw2):
    a, gate = jnp.split(h @ w_pw1 + b_pw1, 2, axis=-1)
    g = a * jax.nn.sigmoid(gate)
    full = jnp.concatenate([state.astype(g.dtype), g], axis=1)
    c = lax.conv_general_dilated(full, dw_w[:, None, :].astype(full.dtype), window_strides=(1,),
                                 padding='VALID', dimension_numbers=('NWC', 'WIO', 'NWC'),
                                 feature_group_count=D_MODEL) + dw_b
    c = jax.nn.silu(layernorm(c, ln_g, ln_b))
    return c @ w_pw2, full[:, -(CONV_WIDTH - 1):]


def dsa_project(h, w_in):
    b, t, _ = h.shape
    q, k, v, qi, ki, wi = jnp.split(h @ w_in, SPLITS, axis=-1)
    return (q.reshape(b, t, N_KV_HEADS, GROUP, HEAD_DIM),
            k.reshape(b, t, N_KV_HEADS, HEAD_DIM),
            v.reshape(b, t, N_KV_HEADS, HEAD_DIM),
            qi.reshape(b, t, N_IDX_HEADS, IDX_DIM), ki, wi)


def dsa_block(q, qi, wi, q_chunk, k, v, ki, k_chunk, top_k):
    f32 = jnp.float32
    dots = jnp.einsum('bqhd,bsd->bqsh', qi.astype(f32), ki.astype(f32)) * (IDX_DIM ** -0.5)
    score = jnp.einsum('bqsh,bqh->bqs', jax.nn.relu(dots), wi.astype(f32)) * (N_IDX_HEADS ** -0.5)
    admissible = k_chunk[None, :] <= q_chunk[:, None]
    score = jnp.where(admissible[None], score, -jnp.inf)
    _, idx = lax.top_k(score, top_k)
    valid = k_chunk[idx] <= q_chunk[None, :, None]
    k_sel = jax.vmap(lambda kb, ib: kb[ib])(k, idx)
    v_sel = jax.vmap(lambda vb, ib: vb[ib])(v, idx)
    logits = jnp.einsum('bqhgd,bqkhd->bqhgk', q.astype(f32), k_sel.astype(f32)) * (HEAD_DIM ** -0.5)
    logits = jnp.where(valid[:, :, None, None, :], logits, -jnp.inf)
    probs = jax.nn.softmax(logits, axis=-1)
    out = jnp.einsum('bqhgk,bqkhd->bqhgd', probs, v_sel.astype(f32))
    return out.astype(v.dtype)


def dsa_prompt(q, qi, wi, k, v, ki, chunk_ids, top_k):
    b, t = q.shape[:2]
    nb = -(-t // Q_BLOCK)
    pad = nb * Q_BLOCK - t

    def blocks(a):
        a = jnp.pad(a, [(0, 0), (0, pad)] + [(0, 0)] * (a.ndim - 2))
        return jnp.moveaxis(a.reshape((b, nb, Q_BLOCK) + a.shape[2:]), 1, 0)

    qc = jnp.pad(chunk_ids, (0, pad), mode='edge').reshape(nb, Q_BLOCK)

    def one(xs):
        qb, qib, wib, qcb = xs
        return dsa_block(qb, qib, wib, qcb, k, v, ki, chunk_ids, top_k)

    out = lax.map(one, (blocks(q), blocks(qi), blocks(wi), qc))
    return jnp.moveaxis(out, 0, 1).reshape(b, nb * Q_BLOCK, Q_WIDTH)[:, :t]


def moe_ffn(h, w_router, b_router, w_gate, w_up, w_down):
    logits = (h @ w_router).astype(jnp.float32) + b_router.astype(jnp.float32)
    top_val, top_idx = lax.top_k(logits, TOP_K_EXPERTS)
    gates = jax.nn.softmax(top_val, axis=-1)
    combine = jnp.sum(jax.nn.one_hot(top_idx, N_EXPERTS, dtype=jnp.float32) * gates[..., None], axis=-2)
    out = jnp.zeros_like(h)
    for e in range(N_EXPERTS):
        out = out + combine[..., e:e + 1].astype(h.dtype) * swiglu(h, w_gate[e], w_up[e], w_down[e])
    return out


def setup_inputs(seed: int = 0) -> dict:
    key = jax.random.key(seed)
    ks = jax.random.split(key, 32)
    f32 = jnp.float32

    def nrm(k, shape, scale):
        return jax.random.normal(k, shape, f32) * scale

    def gain(k, shape):
        return 1.0 + 0.05 * jax.random.normal(k, shape, f32)

    NC, NA = N_CONV_LAYERS, N_ATTN_LAYERS
    return {
        'x_prompt': nrm(ks[0], (BATCH, SEQ, D_MODEL), 1.0),
        'x_sample': nrm(ks[1], (DEC_BATCH, DEC_SEQ, D_MODEL), 1.0),
        'cache_conv': nrm(ks[2], (NC, DEC_BATCH, CONV_WIDTH - 1, D_MODEL), 0.5),
        'cache_k': nrm(ks[3], (NA, DEC_BATCH, PAST_LEN, N_KV_HEADS, HEAD_DIM), 1.0),
        'cache_v': nrm(ks[4], (NA, DEC_BATCH, PAST_LEN, N_KV_HEADS, HEAD_DIM), 1.0),
        'cache_idx_k': nrm(ks[5], (NA, DEC_BATCH, PAST_LEN, IDX_DIM), 1.0),
        'meta_tokens': nrm(ks[6], (N_META, D_MODEL), 1.0),
        'norm_mix_g': gain(ks[7], (DEPTH, D_MODEL)),
        'norm_ffn_g': gain(ks[8], (DEPTH, D_MODEL)),
        'conv_w_pw1': nrm(ks[9], (NC, D_MODEL, 2 * D_MODEL), D_MODEL ** -0.5),
        'conv_b_pw1': nrm(ks[10], (NC, 2 * D_MODEL), 0.02),
        'conv_dw_w': nrm(ks[11], (NC, CONV_WIDTH, D_MODEL), CONV_WIDTH ** -0.5),
        'conv_dw_b': nrm(ks[12], (NC, D_MODEL), 0.02),
        'conv_ln_g': gain(ks[13], (NC, D_MODEL)),
        'conv_ln_b': nrm(ks[14], (NC, D_MODEL), 0.02),
        'conv_w_pw2': nrm(ks[15], (NC, D_MODEL, D_MODEL), D_MODEL ** -0.5),
        'attn_w_in': nrm(ks[16], (NA, D_MODEL, P_IN), D_MODEL ** -0.5),
        'attn_w_out': nrm(ks[17], (NA, Q_WIDTH, D_MODEL), Q_WIDTH ** -0.5),
        'ffn_w_gate': nrm(ks[18], (NC, D_MODEL, D_FF), D_MODEL ** -0.5),
        'ffn_w_up': nrm(ks[19], (NC, D_MODEL, D_FF), D_MODEL ** -0.5),
        'ffn_w_down': nrm(ks[20], (NC, D_FF, D_MODEL), D_FF ** -0.5),
        'moe_w_router': nrm(ks[21], (NA, D_MODEL, N_EXPERTS), D_MODEL ** -0.5),
        'moe_b_router': nrm(ks[22], (NA, N_EXPERTS), 0.01),
        'moe_w_gate': nrm(ks[23], (NA, N_EXPERTS, D_MODEL, D_FF), D_MODEL ** -0.5),
        'moe_w_up': nrm(ks[24], (NA, N_EXPERTS, D_MODEL, D_FF), D_MODEL ** -0.5),
        'moe_w_down': nrm(ks[25], (NA, N_EXPERTS, D_FF, D_MODEL), D_FF ** -0.5),
        'final_norm_g': gain(ks[26], (D_MODEL,)),
    }


def reference(x_prompt, x_sample, cache_conv, cache_k, cache_v, cache_idx_k,
              meta_tokens, norm_mix_g, norm_ffn_g,
              conv_w_pw1, conv_b_pw1, conv_dw_w, conv_dw_b, conv_ln_g, conv_ln_b, conv_w_pw2,
              attn_w_in, attn_w_out,
              ffn_w_gate, ffn_w_up, ffn_w_down,
              moe_w_router, moe_b_router, moe_w_gate, moe_w_up, moe_w_down,
              final_norm_g):
    b_p, seq = x_prompt.shape[:2]
    b_s, t_s = x_sample.shape[:2]
    past = cache_k.shape[2]
    topk_p = min(TOPK_MAX, seq // 4)
    topk_s = min(TOPK_MAX, (past + t_s) // 4)

    xp = jnp.concatenate([jnp.broadcast_to(meta_tokens[None].astype(x_prompt.dtype), (b_p, N_META, D_MODEL)),
                          x_prompt], axis=1)
    pos_p = jnp.arange(N_META + seq)
    chunk_p = jnp.where(pos_p < N_META, 0, (pos_p - N_META) // CHUNK + 1)
    kchunk_s = jnp.arange(past + t_s) // CHUNK + 1
    qchunk_s = kchunk_s[past:]
    xs = x_sample

    conv_p, conv_s, k_p, v_p, ki_p, k_s, v_s, ki_s = [], [], [], [], [], [], [], []
    for i in range(DEPTH):
        j = i // 2
        hp = rmsnorm(xp, norm_mix_g[i])
        hs = rmsnorm(xs, norm_mix_g[i])
        if i % 2 == 0:
            prm = (conv_w_pw1[j], conv_b_pw1[j], conv_dw_w[j], conv_dw_b[j], conv_ln_g[j], conv_ln_b[j], conv_w_pw2[j])
            mp, sp = conv_module(hp, jnp.zeros((b_p, CONV_WIDTH - 1, D_MODEL), hp.dtype), *prm)
            ms, ss = conv_module(hs, cache_conv[j], *prm)
            conv_p.append(sp)
            conv_s.append(ss)
        else:
            q1, k1, v1, qi1, ki1, wi1 = dsa_project(hp, attn_w_in[j])
            mp = dsa_prompt(q1, qi1, wi1, k1, v1, ki1, chunk_p, topk_p) @ attn_w_out[j]
            k_p.append(k1)
            v_p.append(v1)
            ki_p.append(ki1)
            q2, k2, v2, qi2, ki2, wi2 = dsa_project(hs, attn_w_in[j])
            k_all = jnp.concatenate([cache_k[j].astype(k2.dtype), k2], axis=1)
            v_all = jnp.concatenate([cache_v[j].astype(v2.dtype), v2], axis=1)
            ki_all = jnp.concatenate([cache_idx_k[j].astype(ki2.dtype), ki2], axis=1)
            os_ = dsa_block(q2, qi2, wi2, qchunk_s, k_all, v_all, ki_all, kchunk_s, topk_s)
            ms = os_.reshape(b_s, t_s, Q_WIDTH) @ attn_w_out[j]
            k_s.append(k2)
            v_s.append(v2)
            ki_s.append(ki2)
        xp = xp + mp
        xs = xs + ms
        hp = rmsnorm(xp, norm_ffn_g[i])
        hs = rmsnorm(xs, norm_ffn_g[i])
        if i % 2 == 0:
            xp = xp + swiglu(hp, ffn_w_gate[j], ffn_w_up[j], ffn_w_down[j])
            xs = xs + swiglu(hs, ffn_w_gate[j], ffn_w_up[j], ffn_w_down[j])
        else:
            mo = (moe_w_router[j], moe_b_router[j], moe_w_gate[j], moe_w_up[j], moe_w_down[j])
            xp = xp + moe_ffn(hp, *mo)
            xs = xs + moe_ffn(hs, *mo)

    y_prompt = rmsnorm(xp, final_norm_g)[:, N_META:]
    y_sample = rmsnorm(xs, final_norm_g)
    return (y_prompt, y_sample, jnp.stack(conv_p), jnp.stack(conv_s),
            jnp.stack(k_p), jnp.stack(v_p), jnp.stack(ki_p),
            jnp.stack(k_s), jnp.stack(v_s), jnp.stack(ki_s))
```

```python
import functools

import jax
import jax.numpy as jnp
from jax import lax
from jax.experimental import pallas as pl
from jax.experimental.pallas import tpu as pltpu

F32 = jnp.float32
BF16 = jnp.bfloat16
I32 = jnp.int32

RMS_EPS = 1e-6
LN_EPS = 1e-5
CHUNK = 64
KEY_BLOCK = 128
KEY_PAIR = 2 * KEY_BLOCK
PROMPT_LEAD = 256
MOE_ROWS = 256
NEG_KEY = -0x7F800000
MASKED = -1e30
VMEM_LIMIT = 56 * 1024 * 1024


def _params(*sem, vmem=VMEM_LIMIT):
    return pltpu.CompilerParams(dimension_semantics=sem, vmem_limit_bytes=vmem)


def _const(shape):
    nd = len(shape)
    return pl.BlockSpec(shape, lambda *_: (0,) * nd, pipeline_mode=pl.Buffered(1))


def _rms(x, g):
    ms = jnp.mean(x * x, axis=-1, keepdims=True)
    return x * lax.rsqrt(ms + RMS_EPS) * g


def _dot(a, b):
    return jnp.dot(a, b, preferred_element_type=F32)


def _dot_nt(a, b):
    return lax.dot_general(a, b, (((1,), (1,)), ((), ())), preferred_element_type=F32)


def _row_tile(n):
    for t in (1024, 512, 256, 128, 64):
        if n % t == 0:
            return t
    raise ValueError(f"row count {n} has no supported tile")


def _glu_kernel(x_ref, g_ref, w_ref, b_ref, o_ref):
    d = o_ref.shape[-1]
    h = _rms(x_ref[...], g_ref[...]).astype(BF16)
    ag = _dot(h, w_ref[...]) + b_ref[...]
    o_ref[...] = ag[:, :d] * jax.nn.sigmoid(ag[:, d:])


def _glu(x, g, w, b):
    n, d = x.shape
    tm = _row_tile(n)
    return pl.pallas_call(
        _glu_kernel,
        grid=(n // tm,),
        in_specs=[pl.BlockSpec((tm, d), lambda i: (i, 0)), _const((1, d)), _const(w.shape), _const((1, 2 * d))],
        out_specs=pl.BlockSpec((tm, d), lambda i: (i, 0)),
        out_shape=jax.ShapeDtypeStruct((n, d), F32),
        compiler_params=_params("parallel"),
        name="glu",
    )(x, g, w, b)


CONV_ROWS = 16
HALO = 32


def _conv_kernel(*refs, tt, width, n_zero, aliased):
    if aliased:
        refs = refs[1:]
    g_ref, x_ref, st_ref, dww_ref, dwb_ref, lng_ref, lnb_ref, w2_ref, o_ref, gbuf, hbuf = refs
    t = pl.program_id(1)
    d = o_ref.shape[-1]
    lead = HALO - (width - 1)

    @pl.when(t == 0)
    def _():
        gbuf[0:HALO, :] = st_ref[0]

    @pl.when(t > 0)
    def _():
        gbuf[0:HALO, :] = gbuf[tt:tt + HALO, :]

    gbuf[HALO:HALO + tt, :] = g_ref[...]
    if n_zero:
        @pl.when(t == 0)
        def _():
            gbuf[HALO:HALO + n_zero, :] = jnp.zeros((n_zero, d), F32)

    def chunk(i, carry):
        r0 = pl.multiple_of(i * CONV_ROWS, CONV_ROWS)
        win = gbuf[pl.ds(r0, CONV_ROWS + HALO), :]
        acc = jnp.zeros((CONV_ROWS, d), F32) + dwb_ref[...]
        for k in range(width):
            acc = acc + win[lead + k:lead + k + CONV_ROWS, :] * dww_ref[k:k + 1, :]
        mu = jnp.mean(acc, axis=-1, keepdims=True)
        xc = acc - mu
        var = jnp.mean(xc * xc, axis=-1, keepdims=True)
        y = xc * lax.rsqrt(var + LN_EPS) * lng_ref[...] + lnb_ref[...]
        hbuf[pl.ds(r0, CONV_ROWS), :] = (y * jax.nn.sigmoid(y)).astype(BF16)
        return carry

    lax.fori_loop(0, tt // CONV_ROWS, chunk, 0)
    o_ref[...] = x_ref[...] + _dot(hbuf[...], w2_ref[...])


def _conv(g, x, state, dww, dwb, lng, lnb, w2, *, row0, batch, t_len, tt, n_zero, prev=None):
    n, d = x.shape
    width = dww.shape[0]
    nt = t_len // tt
    blk0 = row0 // tt
    rows = pl.BlockSpec((tt, d), lambda b, t: (blk0 + b * nt + t, 0))
    in_specs = [rows, rows, pl.BlockSpec((1, HALO, d), lambda b, t: (b, 0, 0)),
                _const(dww.shape), _const((1, d)), _const((1, d)), _const((1, d)), _const(w2.shape)]
    args = [g, x, state, dww, dwb, lng, lnb, w2]
    aliases = {}
    if prev is not None:
        in_specs = [pl.BlockSpec(memory_space=pl.ANY)] + in_specs
        args = [prev] + args
        aliases = {0: 0}
    return pl.pallas_call(
        functools.partial(_conv_kernel, tt=tt, width=width, n_zero=n_zero, aliased=prev is not None),
        grid=(batch, nt),
        in_specs=in_specs,
        out_specs=rows,
        out_shape=jax.ShapeDtypeStruct((n, d), F32),
        scratch_shapes=[pltpu.VMEM((tt + HALO, d), F32), pltpu.VMEM((tt, d), BF16)],
        input_output_aliases=aliases,
        compiler_params=_params("arbitrary", "arbitrary"),
        name="conv_mixer",
    )(*args)


def _ffn_kernel(x_ref, g_ref, wg_ref, wu_ref, wd_ref, o_ref, *, fc):
    x = x_ref[...]
    h = _rms(x, g_ref[...]).astype(BF16)
    acc = x
    for c in range(wg_ref.shape[1] // fc):
        a = _dot(h, wg_ref[:, c * fc:(c + 1) * fc])
        u = _dot(h, wu_ref[:, c * fc:(c + 1) * fc])
        z = (a * jax.nn.sigmoid(a) * u).astype(BF16)
        acc = acc + _dot(z, wd_ref[c * fc:(c + 1) * fc, :])
    o_ref[...] = acc


def _ff_chunk(dff):
    for parts in (2, 1, 4, 11, 22):
        if dff % parts == 0 and (dff // parts) % 128 == 0:
            return dff // parts
    return dff


def _ffn(x, g, wg, wu, wd):
    n, d = x.shape
    tm = min(_row_tile(n), 512)
    return pl.pallas_call(
        functools.partial(_ffn_kernel, fc=_ff_chunk(wg.shape[1])),
        grid=(n // tm,),
        in_specs=[pl.BlockSpec((tm, d), lambda i: (i, 0)), _const((1, d)),
                  _const(wg.shape), _const(wu.shape), _const(wd.shape)],
        out_specs=pl.BlockSpec((tm, d), lambda i: (i, 0)),
        out_shape=jax.ShapeDtypeStruct((n, d), F32),
        compiler_params=_params("parallel"),
        name="ffn",
    )(x, g, wg, wu, wd)


def _proj_kernel(x_ref, g_ref, w_ref, q_ref, kv_ref, qi_ref, kw_ref, *, qw, kvw, qiw, q_scale, qi_scale):
    h = _rms(x_ref[...], g_ref[...]).astype(BF16)
    r = _dot(h, w_ref[...])
    q_ref[...] = (r[:, :qw] * q_scale).astype(BF16)
    kv_ref[...] = r[:, qw:qw + kvw]
    qi_ref[...] = (r[:, qw + kvw:qw + kvw + qiw] * qi_scale).astype(BF16)
    kw_ref[...] = r[:, qw + kvw + qiw:]


def _proj(x, g, w, *, qw, kvw, qiw, q_scale, qi_scale):
    n, d = x.shape
    tm = min(_row_tile(n), 512)
    kww = w.shape[1] - qw - kvw - qiw
    row = lambda width: pl.BlockSpec((tm, width), lambda i: (i, 0))
    return pl.pallas_call(
        functools.partial(_proj_kernel, qw=qw, kvw=kvw, qiw=qiw, q_scale=q_scale, qi_scale=qi_scale),
        grid=(n // tm,),
        in_specs=[row(d), _const((1, d)), _const(w.shape)],
        out_specs=[row(qw), row(kvw), row(qiw), row(kww)],
        out_shape=[jax.ShapeDtypeStruct((n, qw), BF16), jax.ShapeDtypeStruct((n, kvw), F32),
                   jax.ShapeDtypeStruct((n, qiw), BF16), jax.ShapeDtypeStruct((n, kww), F32)],
        compiler_params=_params("parallel"),
        name="attn_proj",
    )(x, g, w)


def _attn_kernel(q_ref, k_ref, v_ref, qi_ref, ki_ref, wi_ref, o_ref, keys, wib, m_s, l_s, acc_s,
                 *, causal, s_lo, s_hi, topk, tq, n_blocks, idx_scale):
    n_kv, group = q_ref.shape[1], q_ref.shape[2]
    n_idx = qi_ref.shape[1]
    qt = pl.program_id(1)
    n_live = qt + 1 if causal else n_blocks
    n_pairs = (n_live + 1) // 2

    wi = wi_ref[0]
    for h in range(n_idx):
        wib[h] = jnp.broadcast_to(wi[:, h:h + 1], (tq, KEY_PAIR))

    def score_pair(p, carry):
        s0 = pl.multiple_of(p * KEY_PAIR, KEY_PAIR)
        kt = ki_ref[0, pl.ds(s0, KEY_PAIR), :]
        sc = jnp.zeros((tq, KEY_PAIR), F32)
        for h in range(n_idx):
            sc = sc + jnp.maximum(_dot_nt(qi_ref[0, h], kt), 0.0) * wib[h]
        bits = pltpu.bitcast(sc * idx_scale, I32)
        key = jnp.where(bits < 0, jnp.int32(-2 ** 31) - bits, bits)
        keys[2 * p] = key[:, :KEY_BLOCK]
        keys[2 * p + 1] = key[:, KEY_BLOCK:]
        return carry

    lax.fori_loop(0, n_pairs, score_pair, 0)

    def blank(blk, r0, r1, c0, c1):
        keys[blk, r0:r1, c0:c1] = jnp.full((r1 - r0, c1 - c0), NEG_KEY, I32)

    for a0, a1 in ((0, s_lo), (s_hi, n_blocks * KEY_BLOCK)):
        for blk in range(a0 // KEY_BLOCK, -(-a1 // KEY_BLOCK)):
            c0 = max(a0 - blk * KEY_BLOCK, 0)
            c1 = min(a1 - blk * KEY_BLOCK, KEY_BLOCK)
            if c1 > c0:
                blank(blk, 0, tq, c0, c1)
    if causal:
        blank(qt, 0, CHUNK, CHUNK, KEY_BLOCK)

        @pl.when(qt % 2 == 0)
        def _():
            blank(qt + 1, 0, tq, 0, KEY_BLOCK)

    def count_ge(cand):
        def body(p, acc):
            return (acc + jnp.where(keys[2 * p] >= cand, 1.0, 0.0)
                    + jnp.where(keys[2 * p + 1] >= cand, 1.0, 0.0))
        acc = lax.fori_loop(0, n_pairs, body, jnp.zeros((tq, KEY_BLOCK), F32))
        return jnp.sum(acc, axis=1, keepdims=True)

    def bit_pass(it, lo):
        cand = lo + lax.shift_left(jnp.int32(1), 31 - it)
        return jnp.where(count_ge(cand) >= topk, cand, lo)

    thr = lax.fori_loop(0, 32, bit_pass, jnp.full((tq, KEY_BLOCK), -2 ** 31, I32))
    n_equal_ok = jnp.where(thr[:, :1] == NEG_KEY, 0.0, topk - count_ge(thr + 1))
    thr2 = jnp.concatenate([thr, thr], axis=1)

    ri = lax.broadcasted_iota(I32, (KEY_PAIR, KEY_PAIR), 0)
    ci = lax.broadcasted_iota(I32, (KEY_PAIR, KEY_PAIR), 1)
    before = jnp.where(ri < ci, 1.0, 0.0).astype(BF16)
    m_s[...] = jnp.full(m_s.shape, MASKED, F32)
    l_s[...] = jnp.zeros(l_s.shape, F32)
    acc_s[...] = jnp.zeros(acc_s.shape, F32)

    def attend_pair(p, seen_equal):
        s0 = pl.multiple_of(p * KEY_PAIR, KEY_PAIR)
        kb = jnp.concatenate([keys[2 * p], keys[2 * p + 1]], axis=1)
        eq = kb == thr2
        eqf = jnp.where(eq, 1.0, 0.0)
        rank = _dot(eqf.astype(BF16), before) + seen_equal
        sel = (kb > thr2) | (eq & (rank < n_equal_ok))
        bias = jnp.where(sel, 0.0, MASKED)
        for h in range(n_kv):
            qh = q_ref[0, h].reshape(group * tq, q_ref.shape[-1])
            s = _dot_nt(qh, k_ref[0, h, pl.ds(s0, KEY_PAIR), :])
            s = (s.reshape(group, tq, KEY_PAIR) + bias[None]).reshape(group * tq, KEY_PAIR)
            m_old = m_s[h]
            m_new = jnp.maximum(m_old, jnp.max(s, axis=1, keepdims=True))
            alpha = jnp.exp(m_old - m_new)
            pe = jnp.exp(s - m_new)
            l_s[h] = alpha * l_s[h] + jnp.sum(pe, axis=1, keepdims=True)
            acc_s[h] = alpha * acc_s[h] + _dot(pe.astype(BF16), v_ref[0, h, pl.ds(s0, KEY_PAIR), :])
            m_s[h] = m_new
        return seen_equal + jnp.sum(eqf, axis=1, keepdims=True)

    lax.fori_loop(0, n_pairs, attend_pair, jnp.zeros((tq, 1), F32))
    for h in range(n_kv):
        o = acc_s[h] / l_s[h]
        for g in range(group):
            o_ref[0, h * group + g] = o[g * tq:(g + 1) * tq].astype(BF16)


def _attention(q, k, v, qi, ki, wi, *, causal, s_lo, s_hi, topk, tq, idx_scale):
    b, n_kv, group, t, hd = q.shape
    s = k.shape[2]
    n_idx, di = qi.shape[1], qi.shape[3]
    n_blocks = s // KEY_BLOCK
    assert s % KEY_PAIR == 0 and t % tq == 0
    if causal:
        assert tq == 2 * CHUNK == KEY_BLOCK and s == t
    kernel = functools.partial(_attn_kernel, causal=causal, s_lo=s_lo, s_hi=s_hi, topk=float(topk), tq=tq,
                               n_blocks=n_blocks, idx_scale=idx_scale)
    whole = lambda shape: pl.BlockSpec(shape, lambda bb, qq: (bb,) + (0,) * (len(shape) - 1),
                                       pipeline_mode=pl.Buffered(1))
    return pl.pallas_call(
        kernel,
        grid=(b, t // tq),
        in_specs=[pl.BlockSpec((1, n_kv, group, tq, hd), lambda bb, qq: (bb, 0, 0, qq, 0)),
                  whole((1, n_kv, s, hd)), whole((1, n_kv, s, hd)),
                  pl.BlockSpec((1, n_idx, tq, di), lambda bb, qq: (bb, 0, qq, 0)),
                  whole((1, s, di)),
                  pl.BlockSpec((1, tq, n_idx), lambda bb, qq: (bb, qq, 0))],
        out_specs=pl.BlockSpec((1, n_kv * group, tq, hd), lambda bb, qq: (bb, 0, qq, 0)),
        out_shape=jax.ShapeDtypeStruct((b, n_kv * group, t, hd), BF16),
        scratch_shapes=[pltpu.VMEM((n_blocks, tq, KEY_BLOCK), I32),
                        pltpu.VMEM((n_idx, tq, KEY_PAIR), F32),
                        pltpu.VMEM((n_kv, group * tq, 1), F32),
                        pltpu.VMEM((n_kv, group * tq, 1), F32),
                        pltpu.VMEM((n_kv, group * tq, hd), F32)],
        compiler_params=_params("parallel", "arbitrary"),
        name="dsa_attention",
    )(q, k, v, qi, ki, wi)


def _oproj_kernel(a_ref, x_ref, w_ref, o_ref):
    o_ref[...] = x_ref[...] + _dot(a_ref[...], w_ref[...])


def _oproj(a, x, w):
    n, d = x.shape
    tm = _row_tile(n)
    return pl.pallas_call(
        _oproj_kernel,
        grid=(n // tm,),
        in_specs=[pl.BlockSpec((tm, a.shape[1]), lambda i: (i, 0)), pl.BlockSpec((tm, d), lambda i: (i, 0)),
                  _const(w.shape)],
        out_specs=pl.BlockSpec((tm, d), lambda i: (i, 0)),
        out_shape=jax.ShapeDtypeStruct((n, d), F32),
        compiler_params=_params("parallel"),
        name="attn_out",
    )(a, x, w)


def _router_kernel(x_ref, g_ref, wr_ref, br_ref, h_ref, comb_ref, pos_ref):
    tm = x_ref.shape[0]
    n_e = wr_ref.shape[0]
    hf = _rms(x_ref[...], g_ref[...])
    h_ref[...] = hf.astype(BF16)
    lg = lax.dot_general(wr_ref[...], hf, (((1,), (1,)), ((), ())), precision=lax.Precision.HIGHEST,
                         preferred_element_type=F32) + br_ref[...]
    ids = lax.broadcasted_iota(I32, (n_e, tm), 0).astype(F32)
    m1 = jnp.max(lg, axis=0, keepdims=True)
    i1 = jnp.min(jnp.where(lg == m1, ids, float(n_e)), axis=0, keepdims=True)
    first = ids == i1
    lg2 = jnp.where(first, -jnp.inf, lg)
    m2 = jnp.max(lg2, axis=0, keepdims=True)
    i2 = jnp.min(jnp.where(lg2 == m2, ids, float(n_e)), axis=0, keepdims=True)
    second = ids == i2
    e2 = jnp.exp(m2 - m1)
    den = 1.0 + e2
    comb_ref[...] = jnp.where(first, 1.0 / den, jnp.where(second, e2 / den, 0.0))
    sel = jnp.where(first | second, 1.0, 0.0)
    ri = lax.broadcasted_iota(I32, (tm, tm), 0)
    ci = lax.broadcasted_iota(I32, (tm, tm), 1)
    before = jnp.where(ri < ci, 1.0, 0.0).astype(BF16)
    slot = _dot(sel.astype(BF16), before)
    pos_ref[...] = jnp.where(sel > 0.0, slot, -1.0)


def _router(x, g, wr_t, br, tm):
    n, d = x.shape
    n_e = wr_t.shape[0]
    return pl.pallas_call(
        _router_kernel,
        grid=(n // tm,),
        in_specs=[pl.BlockSpec((tm, d), lambda i: (i, 0)), _const((1, d)), _const(wr_t.shape), _const((n_e, 1))],
        out_specs=[pl.BlockSpec((tm, d), lambda i: (i, 0)), pl.BlockSpec((n_e, tm), lambda i: (0, i)),
                   pl.BlockSpec((n_e, tm), lambda i: (0, i))],
        out_shape=[jax.ShapeDtypeStruct((n, d), BF16), jax.ShapeDtypeStruct((n_e, n), F32),
                   jax.ShapeDtypeStruct((n_e, n), F32)],
        compiler_params=_params("parallel"),
        name="moe_router",
    )(x, g, wr_t, br)


def _moe_kernel(cnt_ref, h_ref, x_ref, post_ref, pos_ref, comb_ref, wg_ref, wu_ref, wd_ref, gf_ref, o_ref,
                xg, yacc, *, n_e, n_c):
    tm = h_ref.shape[0]
    i, e, c = pl.program_id(0), pl.program_id(1), pl.program_id(2)
    n_sub = (cnt_ref[i * n_e + e] + (MOE_ROWS - 1)) // MOE_ROWS

    def rows(r):
        return pl.ds(pl.multiple_of(r * MOE_ROWS, MOE_ROWS), MOE_ROWS)

    @pl.when((e == 0) & (c == 0))
    def _():
        o_ref[...] = x_ref[...]

    @pl.when(c == 0)
    def _():
        slot_of_token = post_ref[0]

        def gather(r, carry):
            want = lax.broadcasted_iota(I32, (MOE_ROWS, tm), 0) + r * MOE_ROWS
            pick = jnp.where(slot_of_token == want, 1.0, 0.0).astype(BF16)
            xg[rows(r), :] = _dot(pick, h_ref[...]).astype(BF16)
            return carry

        lax.fori_loop(0, n_sub, gather, 0)

    def expert(r, carry):
        xr = xg[rows(r), :]
        a = _dot(xr, wg_ref[0])
        u = _dot(xr, wu_ref[0])
        y = _dot((a * jax.nn.sigmoid(a) * u).astype(BF16), wd_ref[0])

        @pl.when(c == 0)
        def _():
            yacc[rows(r), :] = y

        @pl.when(c > 0)
        def _():
            yacc[rows(r), :] += y

        return carry

    lax.fori_loop(0, n_sub, expert, 0)

    @pl.when(c == n_c - 1)
    def _():
        onehot = jnp.where(lax.broadcasted_iota(I32, (1, n_e), 1) == e, 1.0, 0.0)
        slot_col = jnp.sum(pos_ref[...] * onehot, axis=1, keepdims=True)
        gate_col = jnp.sum(comb_ref[...] * onehot, axis=1, keepdims=True)

        def scatter(r, carry):
            want = (lax.broadcasted_iota(I32, (tm, MOE_ROWS), 1) + r * MOE_ROWS).astype(F32)
            place = jnp.where(slot_col == want, 1.0, 0.0).astype(BF16)
            o_ref[...] += gate_col * _dot(place, yacc[rows(r), :].astype(BF16))
            return carry

        lax.fori_loop(0, n_sub, scatter, 0)

    @pl.when((e == n_e - 1) & (c == n_c - 1))
    def _():
        o_ref[...] = _rms(o_ref[...], gf_ref[...])


def _moe(counts, h, x, pos_t, pos, comb, wg, wu, wd, gf, tm):
    n, d = x.shape
    n_e, _, dff = wg.shape
    fc = _ff_chunk(dff)
    n_c = dff // fc
    assert tm % MOE_ROWS == 0
    grid_spec = pltpu.PrefetchScalarGridSpec(
        num_scalar_prefetch=1,
        grid=(n // tm, n_e, n_c),
        in_specs=[pl.BlockSpec((tm, d), lambda i, e, c, cnt: (i, 0)),
                  pl.BlockSpec((tm, d), lambda i, e, c, cnt: (i, 0), pipeline_mode=pl.Buffered(1)),
                  pl.BlockSpec((1, 1, tm), lambda i, e, c, cnt: (e, 0, i)),
                  pl.BlockSpec((tm, n_e), lambda i, e, c, cnt: (i, 0)),
                  pl.BlockSpec((tm, n_e), lambda i, e, c, cnt: (i, 0)),
                  pl.BlockSpec((1, d, fc), lambda i, e, c, cnt: (e, 0, c)),
                  pl.BlockSpec((1, d, fc), lambda i, e, c, cnt: (e, 0, c)),
                  pl.BlockSpec((1, fc, d), lambda i, e, c, cnt: (e, c, 0)),
                  pl.BlockSpec((1, d), lambda i, e, c, cnt: (0, 0))],
        out_specs=pl.BlockSpec((tm, d), lambda i, e, c, cnt: (i, 0)),
        scratch_shapes=[pltpu.VMEM((tm, d), BF16), pltpu.VMEM((tm, d), F32)],
    )
    return pl.pallas_call(
        functools.partial(_moe_kernel, n_e=n_e, n_c=n_c),
        grid_spec=grid_spec,
        out_shape=jax.ShapeDtypeStruct((n, d), F32),
        compiler_params=_params("parallel", "arbitrary", "arbitrary", vmem=60 * 1024 * 1024),
        name="moe_experts",
    )(counts, h, x, pos_t, pos, comb, wg, wu, wd, gf)


def kernel(x_prompt, x_sample, cache_conv, cache_k, cache_v, cache_idx_k, meta_tokens, norm_mix_g, norm_ffn_g, conv_w_pw1, conv_b_pw1, conv_dw_w, conv_dw_b, conv_ln_g, conv_ln_b, conv_w_pw2, attn_w_in, attn_w_out, ffn_w_gate, ffn_w_up, ffn_w_down, moe_w_router, moe_b_router, moe_w_gate, moe_w_up, moe_w_down, final_norm_g):
    bp, seq, d = x_prompt.shape
    bs, ts, _ = x_sample.shape
    past = cache_k.shape[2]
    n_kv, hd = cache_k.shape[3], cache_k.shape[4]
    di = cache_idx_k.shape[3]
    n_meta = meta_tokens.shape[0]
    width = conv_dw_w.shape[1]
    n_e = moe_w_router.shape[2]
    q_width = attn_w_out.shape[1]
    kv_width = n_kv * hd
    n_idx = attn_w_in.shape[2] - q_width - 2 * kv_width - di
    n_idx //= di + 1
    group = q_width // kv_width
    assert width - 1 <= HALO and ts >= width - 1 and seq % KEY_PAIR == 0 and n_meta <= CHUNK
    assert norm_mix_g.shape[0] == 2, "one conv layer followed by one attention layer"

    lead = PROMPT_LEAD - n_meta
    tp = PROMPT_LEAD + seq
    n_p, n_s = bp * tp, bs * ts
    n = n_p + n_s
    row = lambda a: a.reshape(1, -1)
    bf = lambda a: a.astype(BF16)

    head = jnp.concatenate([jnp.zeros((lead, d), F32), meta_tokens.astype(F32)], axis=0)
    xp = jnp.concatenate([jnp.broadcast_to(head[None], (bp, PROMPT_LEAD, d)), x_prompt], axis=1)
    x0 = jnp.concatenate([xp.reshape(n_p, d), x_sample.reshape(n_s, d)], axis=0)

    glu = _glu(x0, row(norm_mix_g[0]), bf(conv_w_pw1[0]), row(conv_b_pw1[0]))
    glu_p = glu[:n_p].reshape(bp, tp, d)
    glu_s = glu[n_p:].reshape(bs, ts, d)
    conv_args = (conv_dw_w[0], row(conv_dw_b[0]), row(conv_ln_g[0]), row(conv_ln_b[0]), bf(conv_w_pw2[0]))
    pad_state = lambda st: jnp.pad(st, ((0, 0), (HALO - (width - 1), 0), (0, 0)))
    tt_p = next(t for t in (768, 512, 256) if tp % t == 0)
    x1 = _conv(glu, x0, pad_state(jnp.zeros((bp, width - 1, d), F32)), *conv_args,
               row0=0, batch=bp, t_len=tp, tt=tt_p, n_zero=lead)
    x1 = _conv(glu, x0, pad_state(cache_conv[0].astype(F32)), *conv_args,
               row0=n_p, batch=bs, t_len=ts, tt=ts, n_zero=0, prev=x1)
    conv_state_p = glu_p[:, tp - (width - 1):]
    conv_state_s = glu_s[:, ts - (width - 1):]
    x2 = _ffn(x1, row(norm_ffn_g[0]), bf(ffn_w_gate[0]), bf(ffn_w_up[0]), bf(ffn_w_down[0]))

    w_in = attn_w_in[0]
    c_q, c_k, c_v, c_qi, c_ki = (q_width, q_width + kv_width, q_width + 2 * kv_width,
                                 q_width + 2 * kv_width + n_idx * di, q_width + 2 * kv_width + n_idx * di + di)
    lane_pad = lambda a: jnp.pad(a, ((0, 0), (0, -a.shape[1] % 128)))
    w_cat = jnp.concatenate([w_in[:, :c_qi], lane_pad(w_in[:, c_qi:c_ki]), lane_pad(w_in[:, c_ki:])], axis=1)
    q, kv, qi, kw = _proj(x2, row(norm_mix_g[1]), bf(w_cat), qw=q_width, kvw=2 * kv_width, qiw=n_idx * di,
                          q_scale=hd ** -0.5, qi_scale=di ** -0.5)
    k_new, v_new = kv[:, :kv_width], kv[:, kv_width:]
    ki_new = kw[:, :di]
    wi_off = -(-di // 128) * 128
    wi = kw[:, wi_off:wi_off + n_idx]

    def heads(a, b, t, nh):
        return a.reshape(b, t, nh, -1).transpose(0, 2, 1, 3)

    def attend(rows, b, t, k_all, v_all, ki_all, **kw_):
        s = k_all.shape[1]
        qh = q[rows].reshape(b, t, n_kv, group, hd).transpose(0, 2, 3, 1, 4)
        o = _attention(qh, heads(bf(k_all).reshape(b * s, -1), b, s, n_kv), heads(bf(v_all).reshape(b * s, -1), b, s, n_kv),
                       heads(qi[rows], b, t, n_idx), bf(ki_all), wi[rows].reshape(b, t, n_idx),
                       idx_scale=n_idx ** -0.5, **kw_)
        return o.transpose(0, 2, 1, 3).reshape(b * t, q_width)

    rows_p, rows_s = slice(0, n_p), slice(n_p, n)
    k_p, v_p, ki_p = (k_new[rows_p].reshape(bp, tp, kv_width), v_new[rows_p].reshape(bp, tp, kv_width),
                      ki_new[rows_p].reshape(bp, tp, di))
    o_p = attend(rows_p, bp, tp, k_p, v_p, ki_p, causal=True, s_lo=lead, s_hi=tp,
                 topk=min(256, seq // 4), tq=2 * CHUNK)
    k_s, v_s, ki_s = (k_new[rows_s].reshape(bs, ts, kv_width), v_new[rows_s].reshape(bs, ts, kv_width),
                      ki_new[rows_s].reshape(bs, ts, di))
    s_all = past + ts
    key_pad = lambda a: jnp.pad(a, ((0, 0), (0, -s_all % KEY_PAIR), (0, 0)))
    o_s = attend(rows_s, bs, ts,
                 key_pad(jnp.concatenate([cache_k[0].reshape(bs, past, kv_width).astype(F32), k_s], axis=1)),
                 key_pad(jnp.concatenate([cache_v[0].reshape(bs, past, kv_width).astype(F32), v_s], axis=1)),
                 key_pad(jnp.concatenate([cache_idx_k[0].astype(F32), ki_s], axis=1)),
                 causal=False, s_lo=0, s_hi=s_all, topk=min(256, s_all // 4), tq=ts)
    x3 = _oproj(jnp.concatenate([o_p, o_s], axis=0), x2, bf(attn_w_out[0]))

    tm = _row_tile(n)
    h, comb_t, pos_t = _router(x3, row(norm_ffn_g[1]), moe_w_router[0].T, moe_b_router[0].reshape(n_e, 1), tm)
    counts = jnp.sum((pos_t >= 0.0).reshape(n_e, n // tm, tm), axis=2, dtype=I32).T.reshape(-1)
    y = _moe(counts, h, x3, pos_t.astype(I32).reshape(n_e, 1, n), pos_t.T, comb_t.T,
             bf(moe_w_gate[0]), bf(moe_w_up[0]), bf(moe_w_down[0]), row(final_norm_g), tm)

    y_prompt = y[:n_p].reshape(bp, tp, d)[:, PROMPT_LEAD:]
    y_sample = y[n_p:].reshape(bs, ts, d)
    unlead = lambda a, *tail: a[:, lead:].reshape((1, bp, n_meta + seq) + tail)
    return (y_prompt, y_sample, conv_state_p[None], conv_state_s[None],
            unlead(k_p, n_kv, hd), unlead(v_p, n_kv, hd), unlead(ki_p, di),
            k_s.reshape(1, bs, ts, n_kv, hd), v_s.reshape(1, bs, ts, n_kv, hd), ki_s.reshape(1, bs, ts, di))
```

```python
import functools

import jax
import jax.numpy as jnp
from jax import lax
from jax.experimental import pallas as pl
from jax.experimental.pallas import tpu as pltpu

F32 = jnp.float32
BF16 = jnp.bfloat16
I32 = jnp.int32

RMS_EPS = 1e-6
LN_EPS = 1e-5
CHUNK = 64
KEY_BLOCK = 128
KEY_TILE = 512
PROMPT_LEAD = 256
MOE_ROWS = 256
MOE_TILES = (896, 1024, 768, 512, 256)
NEG_KEY = -0x7F800000
LOG2_E = 1.4426950408889634
MASKED = -1e30
VMEM_LIMIT = 56 * 1024 * 1024


def _params(*sem, vmem=VMEM_LIMIT):
    return pltpu.CompilerParams(dimension_semantics=sem, vmem_limit_bytes=vmem)


def _const(shape):
    nd = len(shape)
    return pl.BlockSpec(shape, lambda *_: (0,) * nd, pipeline_mode=pl.Buffered(1))


def _rms(x, g):
    ms = jnp.mean(x * x, axis=-1, keepdims=True)
    return x * lax.rsqrt(ms + RMS_EPS) * g


def _dot(a, b):
    return jnp.dot(a, b, preferred_element_type=F32)


def _row_tile(n):
    for t in (1024, 512, 256, 128, 64):
        if n % t == 0:
            return t
    raise ValueError(f"row count {n} has no supported tile")


def _glu_kernel(x_ref, g_ref, w_ref, b_ref, o_ref):
    d = o_ref.shape[-1]
    h = _rms(x_ref[...], g_ref[...]).astype(BF16)
    ag = _dot(h, w_ref[...]) + b_ref[...]
    o_ref[...] = ag[:, :d] * jax.nn.sigmoid(ag[:, d:])


def _glu(x, g, w, b):
    n, d = x.shape
    tm = _row_tile(n)
    return pl.pallas_call(
        _glu_kernel,
        grid=(n // tm,),
        in_specs=[pl.BlockSpec((tm, d), lambda i: (i, 0)), _const((1, d)), _const(w.shape), _const((1, 2 * d))],
        out_specs=pl.BlockSpec((tm, d), lambda i: (i, 0)),
        out_shape=jax.ShapeDtypeStruct((n, d), F32),
        compiler_params=_params("parallel"),
        name="glu",
    )(x, g, w, b)


CONV_ROWS = 32
HALO = 32
STRIPS = 8


def _conv_kernel(*refs, strip, width, n_zero, chained, aliased):
    if aliased:
        refs = refs[1:]
    g_ref, x_ref, st_ref, dww_ref, dwb_ref, lng_ref, lnb_ref, w2_ref, o_ref, ext, cbuf, hbuf, carry = refs
    t = pl.program_id(1)
    d = o_ref.shape[-1]
    lead = HALO - (width - 1)
    nl = d // 128

    def put(s, first, count, rows):
        for lb in range(nl):
            ext[lb, pl.ds((HALO + first) * STRIPS + s, count, stride=STRIPS), :] = rows[:, lb * 128:(lb + 1) * 128]

    if chained:
        @pl.when(t == 0)
        def _():
            carry[...] = st_ref[0]
    for s in range(STRIPS):
        put(s, 0, strip, g_ref[s * strip:(s + 1) * strip, :])
        if not chained:
            put(s, -HALO, HALO, st_ref[s])
        elif s == 0:
            put(s, -HALO, HALO, carry[...])
        else:
            put(s, -HALO, HALO, g_ref[s * strip - HALO:s * strip, :])
    if chained:
        carry[...] = g_ref[STRIPS * strip - HALO:STRIPS * strip, :]
    if n_zero:
        @pl.when(t == 0)
        def _():
            for s in range(STRIPS):
                for first, last in ((-HALO, 0), (0, strip)):
                    count = min(s * strip + last, n_zero) - (s * strip + first)
                    if s * strip + first >= 0 and count > 0:
                        put(s, first, count, jnp.zeros((count, d), F32))

    def chunk(i, c):
        r0 = pl.multiple_of(i * CONV_ROWS, CONV_ROWS)
        for lb in range(nl):
            lanes = slice(lb * 128, (lb + 1) * 128)
            taps = [ext[lb, pl.ds(r0 + (lead + k) * STRIPS, CONV_ROWS), :] * dww_ref[k:k + 1, lanes]
                    for k in range(width)]
            while len(taps) > 1:
                taps = [a + b for a, b in zip(taps[::2], taps[1::2])] + ([taps[-1]] if len(taps) % 2 else [])
            cbuf[lb, pl.ds(r0, CONV_ROWS), :] = taps[0] + dwb_ref[:, lanes]
        return c

    lax.fori_loop(0, strip * STRIPS // CONV_ROWS, chunk, 0)
    for s in range(STRIPS):
        acc = jnp.concatenate([cbuf[lb, pl.ds(s, strip, stride=STRIPS), :] for lb in range(nl)], axis=1)
        mu = jnp.mean(acc, axis=-1, keepdims=True)
        xc = acc - mu
        var = jnp.mean(xc * xc, axis=-1, keepdims=True)
        y = xc * lax.rsqrt(var + LN_EPS) * lng_ref[...] + lnb_ref[...]
        hbuf[s * strip:(s + 1) * strip, :] = (y * jax.nn.sigmoid(y)).astype(BF16)
    o_ref[...] = x_ref[...] + _dot(hbuf[...], w2_ref[...])


def _conv(g, x, state, dww, dwb, lng, lnb, w2, *, row0, groups, steps, strip, n_zero, chained, prev=None):
    n, d = x.shape
    width = dww.shape[0]
    tile = STRIPS * strip
    assert row0 % tile == 0 and strip % CONV_ROWS == 0 and strip >= HALO and (chained or steps == 1)
    blk0 = row0 // tile
    rows = pl.BlockSpec((tile, d), lambda b, t: (blk0 + b * steps + t, 0))
    st_rows = 1 if chained else STRIPS
    in_specs = [rows, rows, pl.BlockSpec((st_rows, HALO, d), lambda b, t: (b, 0, 0)),
                _const(dww.shape), _const((1, d)), _const((1, d)), _const((1, d)), _const(w2.shape)]
    args = [g, x, state, dww, dwb, lng, lnb, w2]
    aliases = {}
    if prev is not None:
        in_specs = [pl.BlockSpec(memory_space=pl.ANY)] + in_specs
        args = [prev] + args
        aliases = {0: 0}
    return pl.pallas_call(
        functools.partial(_conv_kernel, strip=strip, width=width, n_zero=n_zero, chained=chained,
                          aliased=prev is not None),
        grid=(groups, steps),
        in_specs=in_specs,
        out_specs=rows,
        out_shape=jax.ShapeDtypeStruct((n, d), F32),
        scratch_shapes=[pltpu.VMEM((d // 128, (strip + HALO) * STRIPS, 128), F32),
                        pltpu.VMEM((d // 128, tile, 128), F32),
                        pltpu.VMEM((tile, d), BF16), pltpu.VMEM((HALO, d), F32)],
        input_output_aliases=aliases,
        compiler_params=_params("arbitrary", "arbitrary"),
        name="conv_mixer",
    )(*args)


def _ffn_kernel(x_ref, g_ref, wg_ref, wu_ref, wd_ref, o_ref, *, fc):
    x = x_ref[...]
    h = _rms(x, g_ref[...]).astype(BF16)
    acc = x
    for c in range(wg_ref.shape[1] // fc):
        a = _dot(h, wg_ref[:, c * fc:(c + 1) * fc])
        u = _dot(h, wu_ref[:, c * fc:(c + 1) * fc])
        z = (a * jax.nn.sigmoid(a) * u).astype(BF16)
        acc = acc + _dot(z, wd_ref[c * fc:(c + 1) * fc, :])
    o_ref[...] = acc


def _ff_chunk(dff):
    for parts in (2, 1, 4, 11, 22):
        if dff % parts == 0 and (dff // parts) % 128 == 0:
            return dff // parts
    return dff


def _ffn(x, g, wg, wu, wd):
    n, d = x.shape
    tm = min(_row_tile(n), 512)
    return pl.pallas_call(
        functools.partial(_ffn_kernel, fc=_ff_chunk(wg.shape[1])),
        grid=(n // tm,),
        in_specs=[pl.BlockSpec((tm, d), lambda i: (i, 0)), _const((1, d)),
                  _const(wg.shape), _const(wu.shape), _const(wd.shape)],
        out_specs=pl.BlockSpec((tm, d), lambda i: (i, 0)),
        out_shape=jax.ShapeDtypeStruct((n, d), F32),
        compiler_params=_params("parallel"),
        name="ffn",
    )(x, g, wg, wu, wd)


def _proj_kernel(x_ref, g_ref, w_ref, q_ref, kv_ref, qi_ref, kw_ref, *, qw, kvw, qiw, q_scale, qi_scale):
    h = _rms(x_ref[...], g_ref[...]).astype(BF16)
    r = _dot(h, w_ref[...])
    q_ref[...] = (r[:, :qw] * q_scale).astype(BF16)
    kv_ref[...] = r[:, qw:qw + kvw]
    qi_ref[...] = (r[:, qw + kvw:qw + kvw + qiw] * qi_scale).astype(BF16)
    kw_ref[...] = r[:, qw + kvw + qiw:]


def _proj(x, g, w, *, qw, kvw, qiw, q_scale, qi_scale):
    n, d = x.shape
    tm = min(_row_tile(n), 512)
    kww = w.shape[1] - qw - kvw - qiw
    row = lambda width: pl.BlockSpec((tm, width), lambda i: (i, 0))
    return pl.pallas_call(
        functools.partial(_proj_kernel, qw=qw, kvw=kvw, qiw=qiw, q_scale=q_scale, qi_scale=qi_scale),
        grid=(n // tm,),
        in_specs=[row(d), _const((1, d)), _const(w.shape)],
        out_specs=[row(qw), row(kvw), row(qiw), row(kww)],
        out_shape=[jax.ShapeDtypeStruct((n, qw), BF16), jax.ShapeDtypeStruct((n, kvw), F32),
                   jax.ShapeDtypeStruct((n, qiw), BF16), jax.ShapeDtypeStruct((n, kww), F32)],
        compiler_params=_params("parallel"),
        name="attn_proj",
    )(x, g, w)


def _attn_kernel(q_ref, k_ref, v_ref, qi_ref, ki_ref, wi_ref, o_ref, keys, bias_s, qbd, tri, *stats,
                 causal, s_lo, s_hi, topk, tq, t_valid, n_tiles_all, idx_scale):
    n_kv, hd, w = q_ref.shape[1], q_ref.shape[3], q_ref.shape[4]
    n_idx = qi_ref.shape[3] // tq
    blocks = KEY_TILE // KEY_BLOCK
    qt = pl.program_id(1)
    n_tiles = (qt + blocks) // blocks if causal else n_tiles_all
    m_refs, acc_refs = stats[:n_kv], stats[n_kv:]

    wi_row = wi_ref[0, 0]
    qi_t = qi_ref[0, 0]

    def score_tile(p, carry):
        s0 = pl.multiple_of(p * KEY_TILE, KEY_TILE)
        r = jnp.maximum(_dot(ki_ref[0, pl.ds(s0, KEY_TILE), :], qi_t), 0.0) * wi_row
        sc = r[:, :tq]
        for h in range(1, n_idx):
            sc = sc + r[:, h * tq:(h + 1) * tq]
        bits = pltpu.bitcast(sc * idx_scale, I32)
        keys[p] = jnp.where(bits < 0, jnp.int32(-2 ** 31) - bits, bits)
        return carry

    lax.fori_loop(0, n_tiles, score_tile, 0)

    def blank(p, r0, r1, c0, c1):
        keys[p, r0:r1, c0:c1] = jnp.full((r1 - r0, c1 - c0), NEG_KEY, I32)

    for a0, a1 in ((0, s_lo), (s_hi, n_tiles_all * KEY_TILE)):
        for p in range(a0 // KEY_TILE, -(-a1 // KEY_TILE)):
            r0 = max(a0 - p * KEY_TILE, 0)
            r1 = min(a1 - p * KEY_TILE, KEY_TILE)
            if r1 > r0:
                blank(p, r0, r1, 0, tq)
    if causal:
        for r in range(blocks):
            @pl.when(qt % blocks == r)
            def _():
                blank(qt // blocks, r * KEY_BLOCK + CHUNK, (r + 1) * KEY_BLOCK, 0, CHUNK)
                if r < blocks - 1:
                    blank(qt // blocks, (r + 1) * KEY_BLOCK, KEY_TILE, 0, tq)

    fold = 4

    def count_ge(cand):
        def body(p, acc):
            hit = jnp.where(keys[p] >= cand, 1.0, 0.0)
            return acc + jnp.sum(hit.reshape(fold, KEY_TILE // fold, tq), axis=0)
        acc = lax.fori_loop(0, n_tiles, body, jnp.zeros((KEY_TILE // fold, tq), F32))
        return jnp.sum(acc, axis=0, keepdims=True)

    live_query = lax.broadcasted_iota(I32, (1, tq), 1) < (t_valid - qt * tq)

    def some(flags):
        return jnp.max(jnp.where(live_query & flags, 1.0, 0.0)) > 0.0

    def unresolved(state):
        it, _, cnt = state
        return (it < 32) & some(cnt != topk)

    def bit_pass(state):
        it, lo, cnt = state
        cand = lo + lax.shift_left(jnp.int32(1), 31 - it)
        c = count_ge(cand)
        take = c >= topk
        return it + 1, jnp.where(take, cand, lo), jnp.where(take, c, cnt)

    _, thr, cnt = lax.while_loop(unresolved, bit_pass, (jnp.int32(0), jnp.full((1, tq), -2 ** 31, I32),
                                                        jnp.full((1, tq), 3e38, F32)))
    n_equal_ok = jnp.where(thr == NEG_KEY, 0.0, topk - count_ge(thr + 1))
    ties_at_threshold = some((cnt != topk) & (thr != NEG_KEY))

    qbd[...] = jnp.zeros(qbd.shape, BF16)
    for h in range(n_kv):
        qbd[h * hd:(h + 1) * hd, h * w:(h + 1) * w] = q_ref[0, h, 0]
    for r in m_refs:
        r[...] = jnp.full(r.shape, MASKED, F32)
    for r in acc_refs:
        r[...] = jnp.zeros(r.shape, F32)

    def attend(ranked):
        def tile(p, seen_equal):
            s0 = pl.multiple_of(p * KEY_TILE, KEY_TILE)
            kb = keys[p]
            eq = kb == thr
            if ranked:
                eqf = jnp.where(eq, 1.0, 0.0)
                admit = _dot(tri[...], eqf.astype(BF16)) + seen_equal < n_equal_ok
                seen_equal = seen_equal + jnp.sum(eqf, axis=0, keepdims=True)
            else:
                admit = n_equal_ok > 0.0
            bias_s[...] = jnp.where((kb > thr) | (eq & admit), 0.0, MASKED)
            bias = jnp.concatenate([bias_s[...]] * (w // tq), axis=1)
            s_all = _dot(k_ref[0, pl.ds(s0, KEY_TILE), :], qbd[...])
            for h in range(n_kv):
                s = s_all[:, h * w:(h + 1) * w] + bias
                m_old = m_refs[h][...]
                m_new = jnp.maximum(m_old, jnp.max(s, axis=0, keepdims=True))
                pe = jnp.exp2(s - m_new).astype(BF16)
                acc_refs[h][...] = jnp.exp2(m_old - m_new) * acc_refs[h][...] + _dot(v_ref[0, h, p], pe)
                m_refs[h][...] = m_new
            return seen_equal

        lax.fori_loop(0, n_tiles, tile, jnp.zeros((1, tq), F32))

    @pl.when(ties_at_threshold)
    def _():
        ri = lax.broadcasted_iota(I32, (KEY_TILE, KEY_TILE), 0)
        ci = lax.broadcasted_iota(I32, (KEY_TILE, KEY_TILE), 1)
        tri[...] = jnp.where(ci < ri, 1.0, 0.0).astype(BF16)
        attend(True)

    @pl.when(jnp.logical_not(ties_at_threshold))
    def _():
        attend(False)

    for h in range(n_kv):
        acc = acc_refs[h][...]
        o_ref[0, 0, h] = (acc[:hd] / acc[hd:hd + 1]).astype(BF16)


def _attention(q, k, v, qi, ki, wi, *, causal, s_lo, s_hi, topk, tq, t_valid, idx_scale):
    b, n_kv, nq, hd, w = q.shape
    s = k.shape[1]
    di, iw = qi.shape[2], qi.shape[3]
    n_tiles_all = s // KEY_TILE
    assert s % KEY_TILE == 0 and w % tq == 0 and iw % tq == 0
    if causal:
        assert tq == 2 * CHUNK == KEY_BLOCK and s >= nq * tq
    kernel = functools.partial(_attn_kernel, causal=causal, s_lo=s_lo, s_hi=s_hi, topk=float(topk), tq=tq,
                               t_valid=t_valid, n_tiles_all=n_tiles_all, idx_scale=idx_scale)
    whole = lambda shape: pl.BlockSpec(shape, lambda bb, qq: (bb,) + (0,) * (len(shape) - 1),
                                       pipeline_mode=pl.Buffered(1))
    return pl.pallas_call(
        kernel,
        grid=(b, nq),
        in_specs=[pl.BlockSpec((1, n_kv, 1, hd, w), lambda bb, qq: (bb, 0, qq, 0, 0)),
                  whole((1, s, n_kv * hd)), whole((1, n_kv, n_tiles_all, v.shape[3], KEY_TILE)),
                  pl.BlockSpec((1, 1, di, iw), lambda bb, qq: (bb, qq, 0, 0)),
                  whole((1, s, di)),
                  pl.BlockSpec((1, 1, 1, iw), lambda bb, qq: (bb, qq, 0, 0))],
        out_specs=pl.BlockSpec((1, 1, n_kv, hd, w), lambda bb, qq: (bb, qq, 0, 0, 0)),
        out_shape=jax.ShapeDtypeStruct((b, nq, n_kv, hd, w), BF16),
        scratch_shapes=[pltpu.VMEM((n_tiles_all, KEY_TILE, tq), I32),
                        pltpu.VMEM((KEY_TILE, tq), F32),
                        pltpu.VMEM((n_kv * hd, n_kv * w), BF16),
                        pltpu.VMEM((KEY_TILE, KEY_TILE), BF16),
                        *[pltpu.VMEM((1, w), F32)] * n_kv,
                        *[pltpu.VMEM((v.shape[3], w), F32)] * n_kv],
        compiler_params=_params("parallel", "arbitrary"),
        name="dsa_attention",
    )(q, k, v, qi, ki, wi)


def _oproj_kernel(a_ref, x_ref, w_ref, o_ref):
    o_ref[...] = x_ref[...] + _dot(a_ref[...], w_ref[...])


def _oproj(a, x, w):
    n, d = x.shape
    tm = _row_tile(n)
    return pl.pallas_call(
        _oproj_kernel,
        grid=(n // tm,),
        in_specs=[pl.BlockSpec((tm, a.shape[1]), lambda i: (i, 0)), pl.BlockSpec((tm, d), lambda i: (i, 0)),
                  _const(w.shape)],
        out_specs=pl.BlockSpec((tm, d), lambda i: (i, 0)),
        out_shape=jax.ShapeDtypeStruct((n, d), F32),
        compiler_params=_params("parallel"),
        name="attn_out",
    )(a, x, w)


def _router_kernel(x_ref, g_ref, wr_ref, br_ref, h_ref, comb_ref, pos_ref):
    tm = x_ref.shape[0]
    n_e = wr_ref.shape[0]
    hf = _rms(x_ref[...], g_ref[...])
    h_ref[...] = hf.astype(BF16)
    lg = lax.dot_general(wr_ref[...], hf, (((1,), (1,)), ((), ())), precision=lax.Precision.HIGHEST,
                         preferred_element_type=F32) + br_ref[...]
    ids = lax.broadcasted_iota(I32, (n_e, tm), 0).astype(F32)
    m1 = jnp.max(lg, axis=0, keepdims=True)
    i1 = jnp.min(jnp.where(lg == m1, ids, float(n_e)), axis=0, keepdims=True)
    first = ids == i1
    lg2 = jnp.where(first, -jnp.inf, lg)
    m2 = jnp.max(lg2, axis=0, keepdims=True)
    i2 = jnp.min(jnp.where(lg2 == m2, ids, float(n_e)), axis=0, keepdims=True)
    second = ids == i2
    e2 = jnp.exp(m2 - m1)
    den = 1.0 + e2
    comb_ref[...] = jnp.where(first, 1.0 / den, jnp.where(second, e2 / den, 0.0))
    sel = jnp.where(first | second, 1.0, 0.0)
    ri = lax.broadcasted_iota(I32, (tm, tm), 0)
    ci = lax.broadcasted_iota(I32, (tm, tm), 1)
    before = jnp.where(ri < ci, 1.0, 0.0).astype(BF16)
    slot = _dot(sel.astype(BF16), before)
    pos_ref[...] = jnp.where(sel > 0.0, slot, -1.0)


def _router(x, g, wr_t, br, tm):
    n, d = x.shape
    n_e = wr_t.shape[0]
    return pl.pallas_call(
        _router_kernel,
        grid=(n // tm,),
        in_specs=[pl.BlockSpec((tm, d), lambda i: (i, 0)), _const((1, d)), _const(wr_t.shape), _const((n_e, 1))],
        out_specs=[pl.BlockSpec((tm, d), lambda i: (i, 0)), pl.BlockSpec((n_e, tm), lambda i: (0, i)),
                   pl.BlockSpec((n_e, tm), lambda i: (0, i))],
        out_shape=[jax.ShapeDtypeStruct((n, d), BF16), jax.ShapeDtypeStruct((n_e, n), F32),
                   jax.ShapeDtypeStruct((n_e, n), F32)],
        compiler_params=_params("parallel"),
        name="moe_router",
    )(x, g, wr_t, br)


def _moe_kernel(cnt_ref, h_ref, x_ref, post_ref, pos_ref, comb_ref, wg_ref, wu_ref, wd_ref, gf_ref, o_ref,
                xg, yacc, *, n_e, n_c):
    tm = h_ref.shape[0]
    i, e, c = pl.program_id(0), pl.program_id(1), pl.program_id(2)
    n_sub = (cnt_ref[i * n_e + e] + (MOE_ROWS - 1)) // MOE_ROWS

    def rows(r):
        return pl.ds(pl.multiple_of(r * MOE_ROWS, MOE_ROWS), MOE_ROWS)

    @pl.when((e == 0) & (c == 0))
    def _():
        o_ref[...] = x_ref[...]

    @pl.when(c == 0)
    def _():
        slot_of_token = post_ref[0]

        def gather(r, carry):
            want = lax.broadcasted_iota(I32, (MOE_ROWS, tm), 0) + r * MOE_ROWS
            pick = jnp.where(slot_of_token == want, 1.0, 0.0).astype(BF16)
            xg[rows(r), :] = _dot(pick, h_ref[...]).astype(BF16)
            return carry

        lax.fori_loop(0, n_sub, gather, 0)

    def expert(r, carry):
        xr = xg[rows(r), :]
        a = _dot(xr, wg_ref[0])
        u = _dot(xr, wu_ref[0])
        y = _dot((a * jax.nn.sigmoid(a) * u).astype(BF16), wd_ref[0])

        @pl.when(c == 0)
        def _():
            yacc[rows(r), :] = y

        @pl.when(c > 0)
        def _():
            yacc[rows(r), :] += y

        return carry

    lax.fori_loop(0, n_sub, expert, 0)

    @pl.when(c == n_c - 1)
    def _():
        onehot = jnp.where(lax.broadcasted_iota(I32, (1, n_e), 1) == e, 1.0, 0.0)
        slot_col = jnp.sum(pos_ref[...] * onehot, axis=1, keepdims=True)
        gate_col = jnp.sum(comb_ref[...] * onehot, axis=1, keepdims=True)

        def scatter(r, carry):
            want = (lax.broadcasted_iota(I32, (tm, MOE_ROWS), 1) + r * MOE_ROWS).astype(F32)
            place = jnp.where(slot_col == want, 1.0, 0.0).astype(BF16)
            o_ref[...] += gate_col * _dot(place, yacc[rows(r), :].astype(BF16))
            return carry

        lax.fori_loop(0, n_sub, scatter, 0)

    @pl.when((e == n_e - 1) & (c == n_c - 1))
    def _():
        o_ref[...] = _rms(o_ref[...], gf_ref[...])


def _moe(counts, h, x, pos_t, pos, comb, wg, wu, wd, gf, tm):
    n, d = x.shape
    n_e, _, dff = wg.shape
    fc = _ff_chunk(dff)
    n_c = dff // fc
    slots = -(-tm // MOE_ROWS) * MOE_ROWS
    grid_spec = pltpu.PrefetchScalarGridSpec(
        num_scalar_prefetch=1,
        grid=(n // tm, n_e, n_c),
        in_specs=[pl.BlockSpec((tm, d), lambda i, e, c, cnt: (i, 0)),
                  pl.BlockSpec((tm, d), lambda i, e, c, cnt: (i, 0), pipeline_mode=pl.Buffered(1)),
                  pl.BlockSpec((1, 1, tm), lambda i, e, c, cnt: (e, 0, i)),
                  pl.BlockSpec((tm, n_e), lambda i, e, c, cnt: (i, 0)),
                  pl.BlockSpec((tm, n_e), lambda i, e, c, cnt: (i, 0)),
                  pl.BlockSpec((1, d, fc), lambda i, e, c, cnt: (e, 0, c)),
                  pl.BlockSpec((1, d, fc), lambda i, e, c, cnt: (e, 0, c)),
                  pl.BlockSpec((1, fc, d), lambda i, e, c, cnt: (e, c, 0)),
                  pl.BlockSpec((1, d), lambda i, e, c, cnt: (0, 0))],
        out_specs=pl.BlockSpec((tm, d), lambda i, e, c, cnt: (i, 0)),
        scratch_shapes=[pltpu.VMEM((slots, d), BF16), pltpu.VMEM((slots, d), F32)],
    )
    return pl.pallas_call(
        functools.partial(_moe_kernel, n_e=n_e, n_c=n_c),
        grid_spec=grid_spec,
        out_shape=jax.ShapeDtypeStruct((n, d), F32),
        compiler_params=_params("parallel", "arbitrary", "arbitrary", vmem=60 * 1024 * 1024),
        name="moe_experts",
    )(counts, h, x, pos_t, pos, comb, wg, wu, wd, gf)


def kernel(x_prompt, x_sample, cache_conv, cache_k, cache_v, cache_idx_k, meta_tokens, norm_mix_g, norm_ffn_g, conv_w_pw1, conv_b_pw1, conv_dw_w, conv_dw_b, conv_ln_g, conv_ln_b, conv_w_pw2, attn_w_in, attn_w_out, ffn_w_gate, ffn_w_up, ffn_w_down, moe_w_router, moe_b_router, moe_w_gate, moe_w_up, moe_w_down, final_norm_g):
    bp, seq, d = x_prompt.shape
    bs, ts, _ = x_sample.shape
    past = cache_k.shape[2]
    n_kv, hd = cache_k.shape[3], cache_k.shape[4]
    di = cache_idx_k.shape[3]
    n_meta = meta_tokens.shape[0]
    width = conv_dw_w.shape[1]
    n_e = moe_w_router.shape[2]
    q_width = attn_w_out.shape[1]
    kv_width = n_kv * hd
    n_idx = (attn_w_in.shape[2] - q_width - 2 * kv_width - di) // (di + 1)
    group = q_width // kv_width
    assert width - 1 <= HALO and ts >= HALO and bs % STRIPS == 0 and seq % PROMPT_LEAD == 0 and n_meta <= CHUNK
    assert norm_mix_g.shape[0] == 2, "one conv layer followed by one attention layer"

    lead = PROMPT_LEAD - n_meta
    tp = PROMPT_LEAD + seq
    n_p, n_s = bp * tp, bs * ts
    n = n_p + n_s
    row = lambda a: a.reshape(1, -1)
    bf = lambda a: a.astype(BF16)

    head = jnp.concatenate([jnp.zeros((lead, d), F32), meta_tokens.astype(F32)], axis=0)
    xp = jnp.concatenate([jnp.broadcast_to(head[None], (bp, PROMPT_LEAD, d)), x_prompt], axis=1)
    x0 = jnp.concatenate([xp.reshape(n_p, d), x_sample.reshape(n_s, d)], axis=0)

    glu = _glu(x0, row(norm_mix_g[0]), bf(conv_w_pw1[0]), row(conv_b_pw1[0]))
    glu_p = glu[:n_p].reshape(bp, tp, d)
    glu_s = glu[n_p:].reshape(bs, ts, d)
    conv_args = (conv_dw_w[0], row(conv_dw_b[0]), row(conv_ln_g[0]), row(conv_ln_b[0]), bf(conv_w_pw2[0]))
    pad_state = lambda st: jnp.pad(st, ((0, 0), (HALO - (width - 1), 0), (0, 0)))
    strip_p = next(t for t in (96, 64, 32) if tp % (STRIPS * t) == 0)
    x1 = _conv(glu, x0, pad_state(jnp.zeros((bp, width - 1, d), F32)), *conv_args, row0=0, groups=bp,
               steps=tp // (STRIPS * strip_p), strip=strip_p, n_zero=lead, chained=True)
    x1 = _conv(glu, x0, pad_state(cache_conv[0].astype(F32)), *conv_args, row0=n_p, groups=bs // STRIPS,
               steps=1, strip=ts, n_zero=0, chained=False, prev=x1)
    conv_state_p = glu_p[:, tp - (width - 1):]
    conv_state_s = glu_s[:, ts - (width - 1):]
    x2 = _ffn(x1, row(norm_ffn_g[0]), bf(ffn_w_gate[0]), bf(ffn_w_up[0]), bf(ffn_w_down[0]))

    w_in = attn_w_in[0]
    c_qi = q_width + 2 * kv_width + n_idx * di
    lane_pad = lambda a: jnp.pad(a, ((0, 0), (0, -a.shape[1] % 128)))
    w_cat = jnp.concatenate([w_in[:, :c_qi], lane_pad(w_in[:, c_qi:c_qi + di]), lane_pad(w_in[:, c_qi + di:])], axis=1)
    q, kv, qi, kw = _proj(x2, row(norm_mix_g[1]), bf(w_cat), qw=q_width, kvw=2 * kv_width, qiw=n_idx * di,
                          q_scale=hd ** -0.5 * LOG2_E, qi_scale=di ** -0.5)
    k_new, v_new = kv[:, :kv_width], kv[:, kv_width:]
    ki_new = kw[:, :di]
    wi_off = -(-di // 128) * 128
    wi = kw[:, wi_off:wi_off + n_idx]

    def attend(rows, b, t, k_all, v_all, ki_all, *, tq, **kw_):
        s = k_all.shape[1]
        nq = -(-t // tq)
        pad_t = lambda a: jnp.pad(a.reshape(b, t, -1), ((0, 0), (0, nq * tq - t), (0, 0)))
        q_t = pad_t(q[rows]).reshape(b, nq, tq, n_kv, group, hd).transpose(0, 3, 1, 5, 4, 2)
        qi_t = pad_t(qi[rows]).reshape(b, nq, tq, n_idx, di).transpose(0, 1, 4, 3, 2)
        wi_t = pad_t(wi[rows]).reshape(b, nq, tq, n_idx).transpose(0, 1, 3, 2)
        v_t = bf(v_all).reshape(b, s // KEY_TILE, KEY_TILE, n_kv, hd).transpose(0, 3, 1, 4, 2)
        extra = jnp.zeros((b, n_kv, s // KEY_TILE, 16, KEY_TILE), BF16).at[:, :, :, 0, :].set(1.0)
        v_t = jnp.concatenate([v_t, extra], axis=3)
        o = _attention(q_t.reshape(b, n_kv, nq, hd, group * tq), bf(k_all), v_t, qi_t.reshape(b, nq, di, n_idx * tq),
                       bf(ki_all), wi_t.reshape(b, nq, 1, n_idx * tq), tq=tq, t_valid=t,
                       idx_scale=n_idx ** -0.5, **kw_)
        o = o.reshape(b, nq, n_kv, hd, group, tq).transpose(0, 1, 5, 2, 4, 3).reshape(b, nq * tq, q_width)
        return o[:, :t].reshape(b * t, q_width)

    tq = 2 * CHUNK
    rows_p, rows_s = slice(0, n_p), slice(n_p, n)
    k_p, v_p, ki_p = (k_new[rows_p].reshape(bp, tp, kv_width), v_new[rows_p].reshape(bp, tp, kv_width),
                      ki_new[rows_p].reshape(bp, tp, di))
    key_pad = lambda a: jnp.pad(a, ((0, 0), (0, -a.shape[1] % KEY_TILE), (0, 0)))
    o_p = attend(rows_p, bp, tp, key_pad(k_p), key_pad(v_p), key_pad(ki_p), causal=True, s_lo=lead, s_hi=tp,
                 topk=min(256, seq // 4), tq=tq)
    k_s, v_s, ki_s = (k_new[rows_s].reshape(bs, ts, kv_width), v_new[rows_s].reshape(bs, ts, kv_width),
                      ki_new[rows_s].reshape(bs, ts, di))
    s_all = past + ts
    o_s = attend(rows_s, bs, ts,
                 key_pad(jnp.concatenate([cache_k[0].reshape(bs, past, kv_width).astype(F32), k_s], axis=1)),
                 key_pad(jnp.concatenate([cache_v[0].reshape(bs, past, kv_width).astype(F32), v_s], axis=1)),
                 key_pad(jnp.concatenate([cache_idx_k[0].astype(F32), ki_s], axis=1)),
                 causal=False, s_lo=0, s_hi=s_all, topk=min(256, s_all // 4), tq=tq)
    x3 = _oproj(jnp.concatenate([o_p, o_s], axis=0), x2, bf(attn_w_out[0]))

    tm = next((t for t in MOE_TILES if n % t == 0), None) or _row_tile(n)
    h, comb_t, pos_t = _router(x3, row(norm_ffn_g[1]), moe_w_router[0].T, moe_b_router[0].reshape(n_e, 1), tm)
    counts = jnp.sum((pos_t >= 0.0).reshape(n_e, n // tm, tm), axis=2, dtype=I32).T.reshape(-1)
    y = _moe(counts, h, x3, pos_t.astype(I32).reshape(n_e, 1, n), pos_t.T, comb_t.T,
             bf(moe_w_gate[0]), bf(moe_w_up[0]), bf(moe_w_down[0]), row(final_norm_g), tm)

    y_prompt = y[:n_p].reshape(bp, tp, d)[:, PROMPT_LEAD:]
    y_sample = y[n_p:].reshape(bs, ts, d)
    unlead = lambda a, *tail: a[:, lead:].reshape((1, bp, n_meta + seq) + tail)
    return (y_prompt, y_sample, conv_state_p[None], conv_state_s[None],
            unlead(k_p, n_kv, hd), unlead(v_p, n_kv, hd), unlead(ki_p, di),
            k_s.reshape(1, bs, ts, n_kv, hd), v_s.reshape(1, bs, ts, n_kv, hd), ki_s.reshape(1, bs, ts, di))
```

```python
import functools

import jax
import jax.numpy as jnp
from jax import lax
from jax.experimental import pallas as pl
from jax.experimental.pallas import tpu as pltpu

F32 = jnp.float32
BF16 = jnp.bfloat16
I32 = jnp.int32

RMS_EPS = 1e-6
LN_EPS = 1e-5
CHUNK = 64
KEY_BLOCK = 128
KEY_TILE = 512
BITS_PER_CHECK = 4
V_EXTRA = 16
GROUPS = 256
PROMPT_LEAD = 256
MOE_ROWS = 256
MOE_TILES = (896, 1024, 768, 512, 256)
NEG_KEY = -0x7F800000
LOG2_E = 1.4426950408889634
MASKED = -1e30
VMEM_LIMIT = 56 * 1024 * 1024


def _params(*sem, vmem=VMEM_LIMIT):
    return pltpu.CompilerParams(dimension_semantics=sem, vmem_limit_bytes=vmem)


def _const(shape):
    nd = len(shape)
    return pl.BlockSpec(shape, lambda *_: (0,) * nd, pipeline_mode=pl.Buffered(1))


def _rms(x, g):
    ms = jnp.mean(x * x, axis=-1, keepdims=True)
    return x * lax.rsqrt(ms + RMS_EPS) * g


def _dot(a, b):
    return jnp.dot(a, b, preferred_element_type=F32)


def _row_tile(n):
    for t in (1024, 512, 256, 128, 64):
        if n % t == 0:
            return t
    raise ValueError(f"row count {n} has no supported tile")


def _glu_kernel(x_ref, g_ref, w_ref, b_ref, o_ref):
    d = o_ref.shape[-1]
    h = _rms(x_ref[...], g_ref[...]).astype(BF16)
    ag = _dot(h, w_ref[...]) + b_ref[...]
    o_ref[...] = ag[:, :d] * jax.nn.sigmoid(ag[:, d:])


def _glu(x, g, w, b):
    n, d = x.shape
    tm = _row_tile(n)
    return pl.pallas_call(
        _glu_kernel,
        grid=(n // tm,),
        in_specs=[pl.BlockSpec((tm, d), lambda i: (i, 0)), _const((1, d)), _const(w.shape), _const((1, 2 * d))],
        out_specs=pl.BlockSpec((tm, d), lambda i: (i, 0)),
        out_shape=jax.ShapeDtypeStruct((n, d), F32),
        compiler_params=_params("parallel"),
        name="glu",
    )(x, g, w, b)


CONV_ROWS = 32
HALO = 32
STRIPS = 8


def _conv_kernel(g_ref, x_ref, st_ref, dww_ref, dwb_ref, lng_ref, lnb_ref, w2_ref, o_ref, ext, cbuf, hbuf, carry,
                 *, strip, width, n_zero, chained):
    t = pl.program_id(1)
    d = o_ref.shape[-1]
    lead = HALO - (width - 1)
    nl = d // 128

    def put(s, first, count, rows):
        for lb in range(nl):
            ext[lb, pl.ds((HALO + first) * STRIPS + s, count, stride=STRIPS), :] = rows[:, lb * 128:(lb + 1) * 128]

    if chained:
        @pl.when(t == 0)
        def _():
            carry[...] = st_ref[0]
    for s in range(STRIPS):
        put(s, 0, strip, g_ref[s * strip:(s + 1) * strip, :])
        if not chained:
            put(s, -HALO, HALO, st_ref[s])
        elif s == 0:
            put(s, -HALO, HALO, carry[...])
        else:
            put(s, -HALO, HALO, g_ref[s * strip - HALO:s * strip, :])
    if chained:
        carry[...] = g_ref[STRIPS * strip - HALO:STRIPS * strip, :]
    if n_zero:
        @pl.when(t == 0)
        def _():
            for s in range(STRIPS):
                for first, last in ((-HALO, 0), (0, strip)):
                    count = min(s * strip + last, n_zero) - (s * strip + first)
                    if s * strip + first >= 0 and count > 0:
                        put(s, first, count, jnp.zeros((count, d), F32))

    def chunk(i, c):
        r0 = pl.multiple_of(i * CONV_ROWS, CONV_ROWS)
        for lb in range(nl):
            lanes = slice(lb * 128, (lb + 1) * 128)
            taps = [ext[lb, pl.ds(r0 + (lead + k) * STRIPS, CONV_ROWS), :] * dww_ref[k:k + 1, lanes]
                    for k in range(width)]
            while len(taps) > 1:
                taps = [a + b for a, b in zip(taps[::2], taps[1::2])] + ([taps[-1]] if len(taps) % 2 else [])
            cbuf[lb, pl.ds(r0, CONV_ROWS), :] = taps[0] + dwb_ref[:, lanes]
        return c

    lax.fori_loop(0, strip * STRIPS // CONV_ROWS, chunk, 0)
    for s in range(STRIPS):
        acc = jnp.concatenate([cbuf[lb, pl.ds(s, strip, stride=STRIPS), :] for lb in range(nl)], axis=1)
        mu = jnp.mean(acc, axis=-1, keepdims=True)
        xc = acc - mu
        var = jnp.mean(xc * xc, axis=-1, keepdims=True)
        y = xc * lax.rsqrt(var + LN_EPS) * lng_ref[...] + lnb_ref[...]
        hbuf[s * strip:(s + 1) * strip, :] = (y * jax.nn.sigmoid(y)).astype(BF16)
    o_ref[...] = x_ref[...] + _dot(hbuf[...], w2_ref[...])


def _conv(g, x, state, dww, dwb, lng, lnb, w2, *, row0, groups, steps, strip, n_zero, chained):
    n, d = x.shape
    width = dww.shape[0]
    tile = STRIPS * strip
    assert row0 % tile == 0 and strip % CONV_ROWS == 0 and strip >= HALO and (chained or steps == 1)
    blk0 = row0 // tile
    rows = pl.BlockSpec((tile, d), lambda b, t: (blk0 + b * steps + t, 0))
    st_rows = 1 if chained else STRIPS
    in_specs = [rows, rows, pl.BlockSpec((st_rows, HALO, d), lambda b, t: (b, 0, 0)),
                _const(dww.shape), _const((1, d)), _const((1, d)), _const((1, d)), _const(w2.shape)]
    return pl.pallas_call(
        functools.partial(_conv_kernel, strip=strip, width=width, n_zero=n_zero, chained=chained),
        grid=(groups, steps),
        in_specs=in_specs,
        out_specs=rows,
        out_shape=jax.ShapeDtypeStruct((n, d), F32),
        scratch_shapes=[pltpu.VMEM((d // 128, (strip + HALO) * STRIPS, 128), F32),
                        pltpu.VMEM((d // 128, tile, 128), F32),
                        pltpu.VMEM((tile, d), BF16), pltpu.VMEM((HALO, d), F32)],
        input_output_aliases={1: 0},
        compiler_params=_params("arbitrary", "arbitrary"),
        name="conv_mixer",
    )(g, x, state, dww, dwb, lng, lnb, w2)


def _ffn_kernel(x_ref, g_ref, wg_ref, wu_ref, wd_ref, o_ref, *, fc):
    x = x_ref[...]
    h = _rms(x, g_ref[...]).astype(BF16)
    acc = x
    for c in range(wg_ref.shape[1] // fc):
        a = _dot(h, wg_ref[:, c * fc:(c + 1) * fc])
        u = _dot(h, wu_ref[:, c * fc:(c + 1) * fc])
        z = (a * jax.nn.sigmoid(a) * u).astype(BF16)
        acc = acc + _dot(z, wd_ref[c * fc:(c + 1) * fc, :])
    o_ref[...] = acc


def _ff_chunk(dff):
    for parts in (2, 1, 4, 11, 22):
        if dff % parts == 0 and (dff // parts) % 128 == 0:
            return dff // parts
    return dff


def _ffn(x, g, wg, wu, wd):
    n, d = x.shape
    tm = min(_row_tile(n), 512)
    return pl.pallas_call(
        functools.partial(_ffn_kernel, fc=_ff_chunk(wg.shape[1])),
        grid=(n // tm,),
        in_specs=[pl.BlockSpec((tm, d), lambda i: (i, 0)), _const((1, d)),
                  _const(wg.shape), _const(wu.shape), _const(wd.shape)],
        out_specs=pl.BlockSpec((tm, d), lambda i: (i, 0)),
        out_shape=jax.ShapeDtypeStruct((n, d), F32),
        compiler_params=_params("parallel"),
        name="ffn",
    )(x, g, wg, wu, wd)


def _proj_kernel(x_ref, g_ref, w_ref, q_ref, kv_ref, qi_ref, kw_ref, *, qw, kvw, qiw, q_scale, qi_scale):
    h = _rms(x_ref[...], g_ref[...]).astype(BF16)
    r = _dot(h, w_ref[...])
    q_ref[...] = (r[:, :qw] * q_scale).astype(BF16)
    kv_ref[...] = r[:, qw:qw + kvw]
    qi_ref[...] = (r[:, qw + kvw:qw + kvw + qiw] * qi_scale).astype(BF16)
    kw_ref[...] = r[:, qw + kvw + qiw:]


def _stream_tile(row0, nrows, cap=1024):
    return next(t for t in (1024, 512, 256, 128, 64) if t <= cap and row0 % t == 0 and nrows % t == 0)


def _proj(x, g, w, *, row0, nrows, qw, kvw, qiw, q_scale, qi_scale):
    d = x.shape[1]
    tm = _stream_tile(row0, nrows, 512)
    kww = w.shape[1] - qw - kvw - qiw
    row = lambda width: pl.BlockSpec((tm, width), lambda i: (i, 0))
    return pl.pallas_call(
        functools.partial(_proj_kernel, qw=qw, kvw=kvw, qiw=qiw, q_scale=q_scale, qi_scale=qi_scale),
        grid=(nrows // tm,),
        in_specs=[pl.BlockSpec((tm, d), lambda i: (row0 // tm + i, 0)), _const((1, d)), _const(w.shape)],
        out_specs=[row(qw), row(kvw), row(qiw), row(kww)],
        out_shape=[jax.ShapeDtypeStruct((nrows, qw), BF16), jax.ShapeDtypeStruct((nrows, kvw), F32),
                   jax.ShapeDtypeStruct((nrows, qiw), BF16), jax.ShapeDtypeStruct((nrows, kww), F32)],
        compiler_params=_params("parallel"),
        name="attn_proj",
    )(x, g, w)


def _attn_kernel(q_ref, k_ref, v_ref, qi_ref, ki_ref, wi_ref, o_ref, keys, bias_s, qbd, tri, *stats,
                 causal, s_lo, s_hi, topk, tq, t_valid, n_tiles_all, idx_scale):
    hd = v_ref.shape[3] - V_EXTRA
    n_kv = v_ref.shape[1]
    group = q_ref.shape[2] // (n_kv * hd)
    w = group * tq
    di = ki_ref.shape[2]
    n_idx = qi_ref.shape[2] // di
    blocks = KEY_TILE // KEY_BLOCK
    qt = pl.program_id(1)
    n_tiles = (qt + blocks) // blocks if causal else n_tiles_all
    m_refs, acc_refs = stats[:n_kv], stats[n_kv:]

    wi_t = wi_ref[0].T
    wi_row = jnp.concatenate([wi_t[h:h + 1, :] for h in range(n_idx)], axis=1)
    qi_all = qi_ref[0].astype(F32).T
    qi_t = jnp.concatenate([qi_all[h * di:(h + 1) * di, :] for h in range(n_idx)], axis=1).astype(BF16)

    def score_tile(p, carry):
        s0 = pl.multiple_of(p * KEY_TILE, KEY_TILE)
        r = jnp.maximum(_dot(ki_ref[0, pl.ds(s0, KEY_TILE), :], qi_t), 0.0) * wi_row
        sc = r[:, :tq]
        for h in range(1, n_idx):
            sc = sc + r[:, h * tq:(h + 1) * tq]
        bits = pltpu.bitcast(sc * idx_scale, I32)
        keys[p] = jnp.where(bits < 0, jnp.int32(-2 ** 31) - bits, bits)
        return carry

    lax.fori_loop(0, n_tiles, score_tile, 0)

    def blank(p, r0, r1, c0, c1):
        keys[p, r0:r1, c0:c1] = jnp.full((r1 - r0, c1 - c0), NEG_KEY, I32)

    for a0, a1 in ((0, s_lo), (s_hi, n_tiles_all * KEY_TILE)):
        for p in range(a0 // KEY_TILE, -(-a1 // KEY_TILE)):
            r0 = max(a0 - p * KEY_TILE, 0)
            r1 = min(a1 - p * KEY_TILE, KEY_TILE)
            if r1 > r0:
                blank(p, r0, r1, 0, tq)
    if causal:
        for r in range(blocks):
            @pl.when(qt % blocks == r)
            def _():
                blank(qt // blocks, r * KEY_BLOCK + CHUNK, (r + 1) * KEY_BLOCK, 0, CHUNK)
                if r < blocks - 1:
                    blank(qt // blocks, (r + 1) * KEY_BLOCK, KEY_TILE, 0, tq)

    fold = 16

    def count_ge(cand):
        def body(p, acc):
            hit = jnp.where(keys[p] >= cand, 1.0, 0.0)
            return acc + jnp.sum(hit.reshape(fold, KEY_TILE // fold, tq), axis=0)
        acc = lax.fori_loop(0, n_tiles, body, jnp.zeros((KEY_TILE // fold, tq), F32))
        return jnp.sum(acc, axis=0, keepdims=True)

    live_query = lax.broadcasted_iota(I32, (1, tq), 1) < (t_valid - qt * tq)

    def some(flags):
        return jnp.max(jnp.where(live_query & flags, 1.0, 0.0)) > 0.0

    def group_max(p, acc):
        return jnp.maximum(acc, jnp.max(keys[p].reshape(KEY_TILE // GROUPS, GROUPS, tq), axis=0))

    gmax = lax.fori_loop(0, n_tiles, group_max, jnp.full((GROUPS, tq), -2 ** 31, I32))
    upper = jnp.max(gmax, axis=0, keepdims=True)
    lower = jnp.min(gmax, axis=0, keepdims=True)
    open_bits = jnp.where(live_query, (32 - lax.clz(lower ^ upper)).astype(F32), 0.0)
    n_bits_f = jnp.max(open_bits, axis=1, keepdims=True)
    n_bits = jnp.broadcast_to(n_bits_f, (1, tq)).astype(I32)
    one = jnp.ones((1, tq), I32)
    step0 = jnp.where(n_bits > 0, lax.shift_left(one, jnp.maximum(n_bits - 1, 0)), 0)
    lo0 = jnp.where(n_bits >= 32, jnp.int32(-2 ** 31), upper & -lax.shift_left(one, jnp.minimum(n_bits, 31)))

    def unresolved(state):
        it, _, _, cnt = state
        return (it < jnp.max(n_bits_f)) & some(cnt != topk)

    def bit_pass(state):
        it, step, lo, cnt = state
        for _ in range(BITS_PER_CHECK):
            cand = lo + step
            c = count_ge(cand)
            take = (c >= topk) & (step != 0)
            step, lo, cnt = lax.shift_right_logical(step, one), jnp.where(take, cand, lo), jnp.where(take, c, cnt)
        return it + float(BITS_PER_CHECK), step, lo, cnt

    _, _, thr, cnt = lax.while_loop(unresolved, bit_pass, (jnp.float32(0.0), step0, lo0, jnp.full((1, tq), 3e38, F32)))
    n_equal_ok = jnp.where(thr == NEG_KEY, 0.0, topk - count_ge(thr + 1))
    ties_at_threshold = some((cnt != topk) & (thr != NEG_KEY))

    qbd[...] = jnp.zeros(qbd.shape, BF16)
    q_all = q_ref[0].astype(F32).T
    for h in range(n_kv):
        for g in range(group):
            r0 = (h * group + g) * hd
            qbd[h * hd:(h + 1) * hd, h * w + g * tq:h * w + (g + 1) * tq] = q_all[r0:r0 + hd, :].astype(BF16)
    for r in m_refs:
        r[...] = jnp.full(r.shape, MASKED, F32)
    for r in acc_refs:
        r[...] = jnp.zeros(r.shape, F32)

    def attend(ranked):
        def tile(p, seen_equal):
            s0 = pl.multiple_of(p * KEY_TILE, KEY_TILE)
            kb = keys[p]
            eq = kb == thr
            if ranked:
                eqf = jnp.where(eq, 1.0, 0.0)
                admit = _dot(tri[...], eqf.astype(BF16)) + seen_equal < n_equal_ok
                seen_equal = seen_equal + jnp.sum(eqf, axis=0, keepdims=True)
            else:
                admit = n_equal_ok > 0.0
            bias_s[...] = jnp.where((kb > thr) | (eq & admit), 0.0, MASKED)
            bias = jnp.concatenate([bias_s[...]] * (w // tq), axis=1)
            s_all = _dot(k_ref[0, pl.ds(s0, KEY_TILE), :], qbd[...])
            for h in range(n_kv):
                s = s_all[:, h * w:(h + 1) * w] + bias
                m_old = m_refs[h][...]
                m_new = jnp.maximum(m_old, jnp.max(s, axis=0, keepdims=True))
                pe = jnp.exp2(s - m_new).astype(BF16)
                acc_refs[h][...] = jnp.exp2(m_old - m_new) * acc_refs[h][...] + _dot(v_ref[0, h, p], pe)
                m_refs[h][...] = m_new
            return seen_equal

        lax.fori_loop(0, n_tiles, tile, jnp.zeros((1, tq), F32))

    @pl.when(ties_at_threshold)
    def _():
        ri = lax.broadcasted_iota(I32, (KEY_TILE, KEY_TILE), 0)
        ci = lax.broadcasted_iota(I32, (KEY_TILE, KEY_TILE), 1)
        tri[...] = jnp.where(ci < ri, 1.0, 0.0).astype(BF16)
        attend(True)

    @pl.when(jnp.logical_not(ties_at_threshold))
    def _():
        attend(False)

    for h in range(n_kv):
        acc = acc_refs[h][...]
        o = acc[:hd] / acc[hd:hd + 1]
        for g in range(0, group, 2):
            pair = jnp.concatenate([o[:, g * tq:(g + 1) * tq], o[:, (g + 1) * tq:(g + 2) * tq]], axis=0)
            c0 = (h * group + g) * hd
            o_ref[0, :, c0:c0 + 2 * hd] = pair.T.astype(BF16)


def _attention(q, k, v, qi, ki, wi, *, causal, s_lo, s_hi, topk, tq, t_valid, idx_scale):
    b, t, qw = q.shape
    s = k.shape[1]
    n_kv, hd = v.shape[1], v.shape[3] - V_EXTRA
    w = qw // n_kv // hd * tq
    n_tiles_all = s // KEY_TILE
    assert s % KEY_TILE == 0 and t % tq == 0 and topk <= GROUPS and KEY_TILE % GROUPS == 0
    assert tq == 2 * CHUNK == KEY_BLOCK and (qw // n_kv // hd) % 2 == 0 and 2 * hd == 128 and wi.shape[2] == 128
    if causal:
        assert s >= t
    kernel = functools.partial(_attn_kernel, causal=causal, s_lo=s_lo, s_hi=s_hi, topk=float(topk), tq=tq,
                               t_valid=t_valid, n_tiles_all=n_tiles_all, idx_scale=idx_scale)
    whole = lambda shape: pl.BlockSpec(shape, lambda bb, qq: (bb,) + (0,) * (len(shape) - 1),
                                       pipeline_mode=pl.Buffered(1))
    tile = lambda width: pl.BlockSpec((1, tq, width), lambda bb, qq: (bb, qq, 0))
    return pl.pallas_call(
        kernel,
        grid=(b, t // tq),
        in_specs=[tile(qw), whole((1, s, k.shape[2])), whole((1,) + v.shape[1:]), tile(qi.shape[2]),
                  whole((1, s, ki.shape[2])), tile(128)],
        out_specs=tile(qw),
        out_shape=jax.ShapeDtypeStruct((b, t, qw), BF16),
        scratch_shapes=[pltpu.VMEM((n_tiles_all, KEY_TILE, tq), I32),
                        pltpu.VMEM((KEY_TILE, tq), F32),
                        pltpu.VMEM((n_kv * hd, n_kv * w), BF16),
                        pltpu.VMEM((KEY_TILE, KEY_TILE), BF16),
                        *[pltpu.VMEM((1, w), F32)] * n_kv,
                        *[pltpu.VMEM((v.shape[3], w), F32)] * n_kv],
        compiler_params=_params("parallel", "arbitrary"),
        name="dsa_attention",
    )(q, k, v, qi, ki, wi)


def _oproj_kernel(a_ref, x_ref, w_ref, o_ref):
    o_ref[...] = x_ref[...] + _dot(a_ref[...], w_ref[...])


def _oproj(a, x, w, *, row0):
    n, d = x.shape
    nrows = a.shape[0]
    tm = _stream_tile(row0, nrows)
    rows = pl.BlockSpec((tm, d), lambda i: (row0 // tm + i, 0))
    return pl.pallas_call(
        _oproj_kernel,
        grid=(nrows // tm,),
        in_specs=[pl.BlockSpec((tm, a.shape[1]), lambda i: (i, 0)), rows, _const(w.shape)],
        out_specs=rows,
        out_shape=jax.ShapeDtypeStruct((n, d), F32),
        input_output_aliases={1: 0},
        compiler_params=_params("parallel"),
        name="attn_out",
    )(a, x, w)


def _router_kernel(x_ref, g_ref, wr_ref, br_ref, h_ref, comb_ref, pos_ref):
    tm = x_ref.shape[0]
    n_e = wr_ref.shape[0]
    hf = _rms(x_ref[...], g_ref[...])
    h_ref[...] = hf.astype(BF16)
    lg = lax.dot_general(wr_ref[...], hf, (((1,), (1,)), ((), ())), precision=lax.Precision.HIGHEST,
                         preferred_element_type=F32) + br_ref[...]
    ids = lax.broadcasted_iota(I32, (n_e, tm), 0).astype(F32)
    m1 = jnp.max(lg, axis=0, keepdims=True)
    i1 = jnp.min(jnp.where(lg == m1, ids, float(n_e)), axis=0, keepdims=True)
    first = ids == i1
    lg2 = jnp.where(first, -jnp.inf, lg)
    m2 = jnp.max(lg2, axis=0, keepdims=True)
    i2 = jnp.min(jnp.where(lg2 == m2, ids, float(n_e)), axis=0, keepdims=True)
    second = ids == i2
    e2 = jnp.exp(m2 - m1)
    den = 1.0 + e2
    comb_ref[...] = jnp.where(first, 1.0 / den, jnp.where(second, e2 / den, 0.0))
    sel = jnp.where(first | second, 1.0, 0.0)
    ri = lax.broadcasted_iota(I32, (tm, tm), 0)
    ci = lax.broadcasted_iota(I32, (tm, tm), 1)
    before = jnp.where(ri < ci, 1.0, 0.0).astype(BF16)
    slot = _dot(sel.astype(BF16), before)
    pos_ref[...] = jnp.where(sel > 0.0, slot, -1.0)


def _router(x, g, wr_t, br, tm):
    n, d = x.shape
    n_e = wr_t.shape[0]
    return pl.pallas_call(
        _router_kernel,
        grid=(n // tm,),
        in_specs=[pl.BlockSpec((tm, d), lambda i: (i, 0)), _const((1, d)), _const(wr_t.shape), _const((n_e, 1))],
        out_specs=[pl.BlockSpec((tm, d), lambda i: (i, 0)), pl.BlockSpec((n_e, tm), lambda i: (0, i)),
                   pl.BlockSpec((n_e, tm), lambda i: (0, i))],
        out_shape=[jax.ShapeDtypeStruct((n, d), BF16), jax.ShapeDtypeStruct((n_e, n), F32),
                   jax.ShapeDtypeStruct((n_e, n), F32)],
        compiler_params=_params("parallel"),
        name="moe_router",
    )(x, g, wr_t, br)


def _moe_kernel(cnt_ref, h_ref, x_ref, post_ref, pos_ref, comb_ref, wg_ref, wu_ref, wd_ref, gf_ref, o_ref,
                xg, yacc, *, n_e, n_c):
    tm = h_ref.shape[0]
    i, e, c = pl.program_id(0), pl.program_id(1), pl.program_id(2)
    n_sub = (cnt_ref[i * n_e + e] + (MOE_ROWS - 1)) // MOE_ROWS

    def rows(r):
        return pl.ds(pl.multiple_of(r * MOE_ROWS, MOE_ROWS), MOE_ROWS)

    @pl.when((e == 0) & (c == 0))
    def _():
        o_ref[...] = x_ref[...]

    @pl.when(c == 0)
    def _():
        slot_of_token = post_ref[0]

        def gather(r, carry):
            want = lax.broadcasted_iota(I32, (MOE_ROWS, tm), 0) + r * MOE_ROWS
            pick = jnp.where(slot_of_token == want, 1.0, 0.0).astype(BF16)
            xg[rows(r), :] = _dot(pick, h_ref[...]).astype(BF16)
            return carry

        lax.fori_loop(0, n_sub, gather, 0)

    def expert(r, carry):
        xr = xg[rows(r), :]
        a = _dot(xr, wg_ref[0])
        u = _dot(xr, wu_ref[0])
        y = _dot((a * jax.nn.sigmoid(a) * u).astype(BF16), wd_ref[0])

        @pl.when(c == 0)
        def _():
            yacc[rows(r), :] = y

        @pl.when(c > 0)
        def _():
            yacc[rows(r), :] += y

        return carry

    lax.fori_loop(0, n_sub, expert, 0)

    @pl.when(c == n_c - 1)
    def _():
        onehot = jnp.where(lax.broadcasted_iota(I32, (1, n_e), 1) == e, 1.0, 0.0)
        slot_col = jnp.sum(pos_ref[...] * onehot, axis=1, keepdims=True)
        gate_col = jnp.sum(comb_ref[...] * onehot, axis=1, keepdims=True)

        def scatter(r, carry):
            want = (lax.broadcasted_iota(I32, (tm, MOE_ROWS), 1) + r * MOE_ROWS).astype(F32)
            place = jnp.where(slot_col == want, 1.0, 0.0).astype(BF16)
            o_ref[...] += gate_col * _dot(place, yacc[rows(r), :].astype(BF16))
            return carry

        lax.fori_loop(0, n_sub, scatter, 0)

    @pl.when((e == n_e - 1) & (c == n_c - 1))
    def _():
        o_ref[...] = _rms(o_ref[...], gf_ref[...])


def _moe(counts, h, x, pos_t, pos, comb, wg, wu, wd, gf, tm):
    n, d = x.shape
    n_e, _, dff = wg.shape
    fc = _ff_chunk(dff)
    n_c = dff // fc
    slots = -(-tm // MOE_ROWS) * MOE_ROWS
    grid_spec = pltpu.PrefetchScalarGridSpec(
        num_scalar_prefetch=1,
        grid=(n // tm, n_e, n_c),
        in_specs=[pl.BlockSpec((tm, d), lambda i, e, c, cnt: (i, 0)),
                  pl.BlockSpec((tm, d), lambda i, e, c, cnt: (i, 0), pipeline_mode=pl.Buffered(1)),
                  pl.BlockSpec((1, 1, tm), lambda i, e, c, cnt: (e, 0, i)),
                  pl.BlockSpec((tm, n_e), lambda i, e, c, cnt: (i, 0)),
                  pl.BlockSpec((tm, n_e), lambda i, e, c, cnt: (i, 0)),
                  pl.BlockSpec((1, d, fc), lambda i, e, c, cnt: (e, 0, c)),
                  pl.BlockSpec((1, d, fc), lambda i, e, c, cnt: (e, 0, c)),
                  pl.BlockSpec((1, fc, d), lambda i, e, c, cnt: (e, c, 0)),
                  pl.BlockSpec((1, d), lambda i, e, c, cnt: (0, 0))],
        out_specs=pl.BlockSpec((tm, d), lambda i, e, c, cnt: (i, 0)),
        scratch_shapes=[pltpu.VMEM((slots, d), BF16), pltpu.VMEM((slots, d), F32)],
    )
    return pl.pallas_call(
        functools.partial(_moe_kernel, n_e=n_e, n_c=n_c),
        grid_spec=grid_spec,
        out_shape=jax.ShapeDtypeStruct((n, d), F32),
        compiler_params=_params("parallel", "arbitrary", "arbitrary", vmem=60 * 1024 * 1024),
        name="moe_experts",
    )(counts, h, x, pos_t, pos, comb, wg, wu, wd, gf)


def kernel(x_prompt, x_sample, cache_conv, cache_k, cache_v, cache_idx_k, meta_tokens, norm_mix_g, norm_ffn_g, conv_w_pw1, conv_b_pw1, conv_dw_w, conv_dw_b, conv_ln_g, conv_ln_b, conv_w_pw2, attn_w_in, attn_w_out, ffn_w_gate, ffn_w_up, ffn_w_down, moe_w_router, moe_b_router, moe_w_gate, moe_w_up, moe_w_down, final_norm_g):
    bp, seq, d = x_prompt.shape
    bs, ts, _ = x_sample.shape
    past = cache_k.shape[2]
    n_kv, hd = cache_k.shape[3], cache_k.shape[4]
    di = cache_idx_k.shape[3]
    n_meta = meta_tokens.shape[0]
    width = conv_dw_w.shape[1]
    n_e = moe_w_router.shape[2]
    q_width = attn_w_out.shape[1]
    kv_width = n_kv * hd
    n_idx = (attn_w_in.shape[2] - q_width - 2 * kv_width - di) // (di + 1)
    assert width - 1 <= HALO and ts >= HALO and bs % STRIPS == 0 and seq % PROMPT_LEAD == 0 and n_meta <= CHUNK
    assert norm_mix_g.shape[0] == 2, "one conv layer followed by one attention layer"

    lead = PROMPT_LEAD - n_meta
    tp = PROMPT_LEAD + seq
    n_p, n_s = bp * tp, bs * ts
    n = n_p + n_s
    row = lambda a: a.reshape(1, -1)
    bf = lambda a: a.astype(BF16)

    head = jnp.concatenate([jnp.zeros((lead, d), F32), meta_tokens.astype(F32)], axis=0)
    x0 = jnp.concatenate([piece for i in range(bp) for piece in (head, x_prompt[i])] + [x_sample.reshape(n_s, d)],
                         axis=0)

    glu = _glu(x0, row(norm_mix_g[0]), bf(conv_w_pw1[0]), row(conv_b_pw1[0]))
    conv_args = (conv_dw_w[0], row(conv_dw_b[0]), row(conv_ln_g[0]), row(conv_ln_b[0]), bf(conv_w_pw2[0]))
    pad_state = lambda st: jnp.pad(st, ((0, 0), (HALO - (width - 1), 0), (0, 0)))
    strip_p = next(t for t in (96, 64, 32) if tp % (STRIPS * t) == 0)
    x1 = _conv(glu, x0, pad_state(jnp.zeros((bp, width - 1, d), F32)), *conv_args, row0=0, groups=bp,
               steps=tp // (STRIPS * strip_p), strip=strip_p, n_zero=lead, chained=True)
    x1 = _conv(glu, x1, pad_state(cache_conv[0].astype(F32)), *conv_args, row0=n_p, groups=bs // STRIPS,
               steps=1, strip=ts, n_zero=0, chained=False)
    conv_state_p = jnp.stack([glu[(i + 1) * tp - (width - 1):(i + 1) * tp] for i in range(bp)])
    conv_state_s = glu[n_p:].reshape(bs, ts, d)[:, ts - (width - 1):]
    x2 = _ffn(x1, row(norm_ffn_g[0]), bf(ffn_w_gate[0]), bf(ffn_w_up[0]), bf(ffn_w_down[0]))

    w_in = attn_w_in[0]
    c_qi = q_width + 2 * kv_width + n_idx * di
    lane_pad = lambda a: jnp.pad(a, ((0, 0), (0, -a.shape[1] % 128)))
    w_cat = jnp.concatenate([w_in[:, :c_qi], lane_pad(w_in[:, c_qi:c_qi + di]), lane_pad(w_in[:, c_qi + di:])], axis=1)
    wi_off = -(-di // 128) * 128
    assert w_cat.shape[1] - c_qi == wi_off + 128
    tq = 2 * CHUNK

    def stream(row0, b, t, cache, **kw_):
        q, kv, qi, kw = _proj(x2, row(norm_mix_g[1]), bf(w_cat), row0=row0, nrows=b * t, qw=q_width,
                              kvw=2 * kv_width, qiw=n_idx * di, q_scale=hd ** -0.5 * LOG2_E, qi_scale=di ** -0.5)
        new = [a.reshape(b, t, -1) for a in (kv[:, :kv_width], kv[:, kv_width:], kw[:, :di])]
        keys = [a if c is None else jnp.concatenate([c.reshape(b, -1, a.shape[2]).astype(F32), a], axis=1)
                for c, a in zip(cache, new)]
        k_all, v_all, ki_all = [bf(jnp.pad(a, ((0, 0), (0, -a.shape[1] % KEY_TILE), (0, 0)))) for a in keys]
        s = k_all.shape[1]
        v_t = v_all.reshape(b, s // KEY_TILE, KEY_TILE, n_kv, hd).transpose(0, 3, 1, 4, 2)
        ones_row = jnp.zeros((b, n_kv, s // KEY_TILE, V_EXTRA, KEY_TILE), BF16).at[:, :, :, 0, :].set(1.0)
        pad_t = lambda a: jnp.pad(a.reshape(b, t, -1), ((0, 0), (0, -t % tq), (0, 0)))
        o = _attention(pad_t(q), k_all, jnp.concatenate([v_t, ones_row], axis=3), pad_t(qi), ki_all,
                       pad_t(kw[:, wi_off:]), tq=tq, t_valid=t, idx_scale=n_idx ** -0.5, s_hi=keys[0].shape[1], **kw_)
        return o[:, :t].reshape(b * t, q_width), new

    o_p, (k_p, v_p, ki_p) = stream(0, bp, tp, (None, None, None), causal=True, s_lo=lead, topk=min(256, seq // 4))
    o_s, (k_s, v_s, ki_s) = stream(n_p, bs, ts, (cache_k[0], cache_v[0], cache_idx_k[0]), causal=False, s_lo=0,
                                   topk=min(256, (past + ts) // 4))
    x3 = _oproj(o_p, x2, bf(attn_w_out[0]), row0=0)
    x3 = _oproj(o_s, x3, bf(attn_w_out[0]), row0=n_p)

    tm = next((t for t in MOE_TILES if n % t == 0), None) or _row_tile(n)
    h, comb_t, pos_t = _router(x3, row(norm_ffn_g[1]), moe_w_router[0].T, moe_b_router[0].reshape(n_e, 1), tm)
    counts = jnp.sum((pos_t >= 0.0).reshape(n_e, n // tm, tm), axis=2, dtype=I32).T.reshape(-1)
    y = _moe(counts, h, x3, pos_t.astype(I32).reshape(n_e, 1, n), pos_t.T, comb_t.T,
             bf(moe_w_gate[0]), bf(moe_w_up[0]), bf(moe_w_down[0]), row(final_norm_g), tm)

    y_prompt = jnp.stack([y[i * tp + PROMPT_LEAD:(i + 1) * tp] for i in range(bp)])
    y_sample = y[n_p:].reshape(bs, ts, d)
    unlead = lambda a, *tail: a[:, lead:].reshape((1, bp, n_meta + seq) + tail)
    return (y_prompt, y_sample, conv_state_p[None], conv_state_s[None],
            unlead(k_p, n_kv, hd), unlead(v_p, n_kv, hd), unlead(ki_p, di),
            k_s.reshape(1, bs, ts, n_kv, hd), v_s.reshape(1, bs, ts, n_kv, hd), ki_s.reshape(1, bs, ts, di))
```

```python
import functools

import jax
import jax.numpy as jnp
from jax import lax
from jax.experimental import pallas as pl
from jax.experimental.pallas import tpu as pltpu

F32 = jnp.float32
BF16 = jnp.bfloat16
I32 = jnp.int32

RMS_EPS = 1e-6
LN_EPS = 1e-5
CHUNK = 64
KEY_BLOCK = 128
KEY_TILE = 512
BITS_PER_CHECK = 4
V_EXTRA = 16
GROUPS = 256
PROMPT_LEAD = 256
MOE_ROWS = 256
MOE_TILES = (896, 1024, 768, 512, 256)
NEG_KEY = -0x7F800000
LOG2_E = 1.4426950408889634
MASKED = -1e30
VMEM_LIMIT = 56 * 1024 * 1024


def _params(*sem, vmem=VMEM_LIMIT):
    return pltpu.CompilerParams(dimension_semantics=sem, vmem_limit_bytes=vmem)


def _const(shape):
    nd = len(shape)
    return pl.BlockSpec(shape, lambda *_: (0,) * nd, pipeline_mode=pl.Buffered(1))


def _rms(x, g):
    ms = jnp.mean(x * x, axis=-1, keepdims=True)
    return x * lax.rsqrt(ms + RMS_EPS) * g


def _dot(a, b):
    return jnp.dot(a, b, preferred_element_type=F32)


def _row_tile(n):
    for t in (1024, 512, 256, 128, 64):
        if n % t == 0:
            return t
    raise ValueError(f"row count {n} has no supported tile")


def _glu_kernel(x_ref, g_ref, w_ref, b_ref, o_ref):
    d = o_ref.shape[-1]
    h = _rms(x_ref[...], g_ref[...]).astype(BF16)
    ag = _dot(h, w_ref[...]) + b_ref[...]
    o_ref[...] = ag[:, :d] * jax.nn.sigmoid(ag[:, d:])


def _glu(x, g, w, b):
    n, d = x.shape
    tm = _row_tile(n)
    return pl.pallas_call(
        _glu_kernel,
        grid=(n // tm,),
        in_specs=[pl.BlockSpec((tm, d), lambda i: (i, 0)), _const((1, d)), _const(w.shape), _const((1, 2 * d))],
        out_specs=pl.BlockSpec((tm, d), lambda i: (i, 0)),
        out_shape=jax.ShapeDtypeStruct((n, d), F32),
        compiler_params=_params("parallel"),
        name="glu",
    )(x, g, w, b)


CONV_ROWS = 32
HALO = 32
STRIPS = 8


def _conv_kernel(g_ref, x_ref, st_ref, dww_ref, dwb_ref, lng_ref, lnb_ref, w2_ref, o_ref, ext, cbuf, hbuf, carry,
                 *, strip, width, n_zero, chained):
    t = pl.program_id(1)
    d = o_ref.shape[-1]
    lead = HALO - (width - 1)
    nl = d // 128

    def put(s, first, count, rows):
        for lb in range(nl):
            ext[lb, pl.ds((HALO + first) * STRIPS + s, count, stride=STRIPS), :] = rows[:, lb * 128:(lb + 1) * 128]

    if chained:
        @pl.when(t == 0)
        def _():
            carry[...] = st_ref[0]
    for s in range(STRIPS):
        put(s, 0, strip, g_ref[s * strip:(s + 1) * strip, :])
        if not chained:
            put(s, -HALO, HALO, st_ref[s])
        elif s == 0:
            put(s, -HALO, HALO, carry[...])
        else:
            put(s, -HALO, HALO, g_ref[s * strip - HALO:s * strip, :])
    if chained:
        carry[...] = g_ref[STRIPS * strip - HALO:STRIPS * strip, :]
    if n_zero:
        @pl.when(t == 0)
        def _():
            for s in range(STRIPS):
                for first, last in ((-HALO, 0), (0, strip)):
                    count = min(s * strip + last, n_zero) - (s * strip + first)
                    if s * strip + first >= 0 and count > 0:
                        put(s, first, count, jnp.zeros((count, d), F32))

    def chunk(i, c):
        r0 = pl.multiple_of(i * CONV_ROWS, CONV_ROWS)
        for lb in range(nl):
            lanes = slice(lb * 128, (lb + 1) * 128)
            taps = [ext[lb, pl.ds(r0 + (lead + k) * STRIPS, CONV_ROWS), :] * dww_ref[k:k + 1, lanes]
                    for k in range(width)]
            while len(taps) > 1:
                taps = [a + b for a, b in zip(taps[::2], taps[1::2])] + ([taps[-1]] if len(taps) % 2 else [])
            cbuf[lb, pl.ds(r0, CONV_ROWS), :] = taps[0] + dwb_ref[:, lanes]
        return c

    lax.fori_loop(0, strip * STRIPS // CONV_ROWS, chunk, 0)
    for s in range(STRIPS):
        acc = jnp.concatenate([cbuf[lb, pl.ds(s, strip, stride=STRIPS), :] for lb in range(nl)], axis=1)
        mu = jnp.mean(acc, axis=-1, keepdims=True)
        xc = acc - mu
        var = jnp.mean(xc * xc, axis=-1, keepdims=True)
        y = xc * lax.rsqrt(var + LN_EPS) * lng_ref[...] + lnb_ref[...]
        hbuf[s * strip:(s + 1) * strip, :] = (y * jax.nn.sigmoid(y)).astype(BF16)
    o_ref[...] = x_ref[...] + _dot(hbuf[...], w2_ref[...])


def _conv(g, x, state, dww, dwb, lng, lnb, w2, *, row0, groups, steps, strip, n_zero, chained):
    n, d = x.shape
    width = dww.shape[0]
    tile = STRIPS * strip
    assert row0 % tile == 0 and strip % CONV_ROWS == 0 and strip >= HALO and (chained or steps == 1)
    blk0 = row0 // tile
    rows = pl.BlockSpec((tile, d), lambda b, t: (blk0 + b * steps + t, 0))
    st_rows = 1 if chained else STRIPS
    in_specs = [rows, rows, pl.BlockSpec((st_rows, HALO, d), lambda b, t: (b, 0, 0)),
                _const(dww.shape), _const((1, d)), _const((1, d)), _const((1, d)), _const(w2.shape)]
    return pl.pallas_call(
        functools.partial(_conv_kernel, strip=strip, width=width, n_zero=n_zero, chained=chained),
        grid=(groups, steps),
        in_specs=in_specs,
        out_specs=rows,
        out_shape=jax.ShapeDtypeStruct((n, d), F32),
        scratch_shapes=[pltpu.VMEM((d // 128, (strip + HALO) * STRIPS, 128), F32),
                        pltpu.VMEM((d // 128, tile, 128), F32),
                        pltpu.VMEM((tile, d), BF16), pltpu.VMEM((HALO, d), F32)],
        input_output_aliases={1: 0},
        compiler_params=_params("arbitrary", "arbitrary"),
        name="conv_mixer",
    )(g, x, state, dww, dwb, lng, lnb, w2)


def _ffn_kernel(x_ref, g_ref, wg_ref, wu_ref, wd_ref, o_ref, *, fc):
    x = x_ref[...]
    h = _rms(x, g_ref[...]).astype(BF16)
    acc = x
    for c in range(wg_ref.shape[1] // fc):
        a = _dot(h, wg_ref[:, c * fc:(c + 1) * fc])
        u = _dot(h, wu_ref[:, c * fc:(c + 1) * fc])
        z = (a * jax.nn.sigmoid(a) * u).astype(BF16)
        acc = acc + _dot(z, wd_ref[c * fc:(c + 1) * fc, :])
    o_ref[...] = acc


def _ff_chunk(dff):
    for parts in (2, 1, 4, 11, 22):
        if dff % parts == 0 and (dff // parts) % 128 == 0:
            return dff // parts
    return dff


def _ffn(x, g, wg, wu, wd):
    n, d = x.shape
    tm = min(_row_tile(n), 512)
    return pl.pallas_call(
        functools.partial(_ffn_kernel, fc=_ff_chunk(wg.shape[1])),
        grid=(n // tm,),
        in_specs=[pl.BlockSpec((tm, d), lambda i: (i, 0)), _const((1, d)),
                  _const(wg.shape), _const(wu.shape), _const(wd.shape)],
        out_specs=pl.BlockSpec((tm, d), lambda i: (i, 0)),
        out_shape=jax.ShapeDtypeStruct((n, d), F32),
        compiler_params=_params("parallel"),
        name="ffn",
    )(x, g, wg, wu, wd)


def _proj_kernel(x_ref, g_ref, w_ref, q_ref, kv_ref, qi_ref, kw_ref, *, qw, kvw, qiw, q_scale, qi_scale):
    h = _rms(x_ref[...], g_ref[...]).astype(BF16)
    r = _dot(h, w_ref[...])
    q_ref[...] = (r[:, :qw] * q_scale).astype(BF16)
    kv_ref[...] = r[:, qw:qw + kvw]
    qi_ref[...] = (r[:, qw + kvw:qw + kvw + qiw] * qi_scale).astype(BF16)
    kw_ref[...] = r[:, qw + kvw + qiw:]


def _stream_tile(row0, nrows, cap=1024):
    return next(t for t in (1024, 512, 256, 128, 64) if t <= cap and row0 % t == 0 and nrows % t == 0)


def _proj(x, g, w, *, row0, nrows, qw, kvw, qiw, q_scale, qi_scale):
    d = x.shape[1]
    tm = _stream_tile(row0, nrows, 512)
    kww = w.shape[1] - qw - kvw - qiw
    row = lambda width: pl.BlockSpec((tm, width), lambda i: (i, 0))
    return pl.pallas_call(
        functools.partial(_proj_kernel, qw=qw, kvw=kvw, qiw=qiw, q_scale=q_scale, qi_scale=qi_scale),
        grid=(nrows // tm,),
        in_specs=[pl.BlockSpec((tm, d), lambda i: (row0 // tm + i, 0)), _const((1, d)), _const(w.shape)],
        out_specs=[row(qw), row(kvw), row(qiw), row(kww)],
        out_shape=[jax.ShapeDtypeStruct((nrows, qw), BF16), jax.ShapeDtypeStruct((nrows, kvw), F32),
                   jax.ShapeDtypeStruct((nrows, qiw), BF16), jax.ShapeDtypeStruct((nrows, kww), F32)],
        compiler_params=_params("parallel"),
        name="attn_proj",
    )(x, g, w)


def _attn_kernel(q_ref, k_ref, v_ref, qi_ref, ki_ref, wi_ref, o_ref, keys, bias_s, qbd, tri, *stats,
                 causal, s_lo, s_hi, topk, tq, t_valid, n_tiles_all, idx_scale):
    hd = v_ref.shape[3] - V_EXTRA
    n_kv = v_ref.shape[1]
    group = q_ref.shape[2] // (n_kv * hd)
    w = group * tq
    di = ki_ref.shape[2]
    n_idx = qi_ref.shape[2] // di
    blocks = KEY_TILE // KEY_BLOCK
    qt = pl.program_id(1)
    n_tiles = (qt + blocks) // blocks if causal else n_tiles_all
    m_refs, acc_refs = stats[:n_kv], stats[n_kv:]

    wi_t = wi_ref[0].T
    wi_row = jnp.concatenate([wi_t[h:h + 1, :] for h in range(n_idx)], axis=1)
    qi_all = qi_ref[0].astype(F32).T
    qi_t = jnp.concatenate([qi_all[h * di:(h + 1) * di, :] for h in range(n_idx)], axis=1).astype(BF16)

    def score_tile(p, carry):
        s0 = pl.multiple_of(p * KEY_TILE, KEY_TILE)
        r = jnp.maximum(_dot(ki_ref[0, pl.ds(s0, KEY_TILE), :], qi_t), 0.0) * wi_row
        sc = r[:, :tq]
        for h in range(1, n_idx):
            sc = sc + r[:, h * tq:(h + 1) * tq]
        bits = pltpu.bitcast(sc * idx_scale, I32)
        keys[p] = jnp.where(bits < 0, jnp.int32(-2 ** 31) - bits, bits)
        return carry

    lax.fori_loop(0, n_tiles, score_tile, 0)

    def blank(p, r0, r1, c0, c1):
        keys[p, r0:r1, c0:c1] = jnp.full((r1 - r0, c1 - c0), NEG_KEY, I32)

    for a0, a1 in ((0, s_lo), (s_hi, n_tiles_all * KEY_TILE)):
        for p in range(a0 // KEY_TILE, -(-a1 // KEY_TILE)):
            r0 = max(a0 - p * KEY_TILE, 0)
            r1 = min(a1 - p * KEY_TILE, KEY_TILE)
            if r1 > r0:
                blank(p, r0, r1, 0, tq)
    if causal:
        for r in range(blocks):
            @pl.when(qt % blocks == r)
            def _():
                blank(qt // blocks, r * KEY_BLOCK + CHUNK, (r + 1) * KEY_BLOCK, 0, CHUNK)
                if r < blocks - 1:
                    blank(qt // blocks, (r + 1) * KEY_BLOCK, KEY_TILE, 0, tq)

    fold = 16

    def count_ge(cand):
        def body(p, acc):
            hit = jnp.where(keys[p] >= cand, 1.0, 0.0)
            return acc + jnp.sum(hit.reshape(fold, KEY_TILE // fold, tq), axis=0)
        acc = lax.fori_loop(0, n_tiles, body, jnp.zeros((KEY_TILE // fold, tq), F32))
        return jnp.sum(acc, axis=0, keepdims=True)

    live_query = lax.broadcasted_iota(I32, (1, tq), 1) < (t_valid - qt * tq)

    def some(flags):
        return jnp.max(jnp.where(live_query & flags, 1.0, 0.0)) > 0.0

    def group_max(p, acc):
        return jnp.maximum(acc, jnp.max(keys[p].reshape(KEY_TILE // GROUPS, GROUPS, tq), axis=0))

    gmax = lax.fori_loop(0, n_tiles, group_max, jnp.full((GROUPS, tq), -2 ** 31, I32))
    upper = jnp.max(gmax, axis=0, keepdims=True)
    lower = jnp.min(gmax, axis=0, keepdims=True)
    open_bits = jnp.where(live_query, (32 - lax.clz(lower ^ upper)).astype(F32), 0.0)
    n_bits_f = jnp.max(open_bits, axis=1, keepdims=True)
    n_bits = jnp.broadcast_to(n_bits_f, (1, tq)).astype(I32)
    one = jnp.ones((1, tq), I32)
    step0 = jnp.where(n_bits > 0, lax.shift_left(one, jnp.maximum(n_bits - 1, 0)), 0)
    lo0 = jnp.where(n_bits >= 32, jnp.int32(-2 ** 31), upper & -lax.shift_left(one, jnp.minimum(n_bits, 31)))

    def unresolved(state):
        it, _, _, cnt = state
        return (it < jnp.max(n_bits_f)) & some(cnt != topk)

    def bit_pass(state):
        it, step, lo, cnt = state
        for _ in range(BITS_PER_CHECK):
            cand = lo + step
            c = count_ge(cand)
            take = (c >= topk) & (step != 0)
            step, lo, cnt = lax.shift_right_logical(step, one), jnp.where(take, cand, lo), jnp.where(take, c, cnt)
        return it + float(BITS_PER_CHECK), step, lo, cnt

    _, _, thr, cnt = lax.while_loop(unresolved, bit_pass, (jnp.float32(0.0), step0, lo0, jnp.full((1, tq), 3e38, F32)))
    n_equal_ok = jnp.where(thr == NEG_KEY, 0.0, topk - count_ge(thr + 1))
    ties_at_threshold = some((cnt != topk) & (thr != NEG_KEY))

    qbd[...] = jnp.zeros(qbd.shape, BF16)
    q_all = q_ref[0].astype(F32).T
    for h in range(n_kv):
        for g in range(group):
            r0 = (h * group + g) * hd
            qbd[h * hd:(h + 1) * hd, h * w + g * tq:h * w + (g + 1) * tq] = q_all[r0:r0 + hd, :].astype(BF16)
    for r in m_refs:
        r[...] = jnp.full(r.shape, MASKED, F32)
    for r in acc_refs:
        r[...] = jnp.zeros(r.shape, F32)

    def attend(ranked):
        def tile(p, seen_equal):
            s0 = pl.multiple_of(p * KEY_TILE, KEY_TILE)
            kb = keys[p]
            eq = kb == thr
            if ranked:
                eqf = jnp.where(eq, 1.0, 0.0)
                admit = _dot(tri[...], eqf.astype(BF16)) + seen_equal < n_equal_ok
                seen_equal = seen_equal + jnp.sum(eqf, axis=0, keepdims=True)
            else:
                admit = n_equal_ok > 0.0
            bias_s[...] = jnp.where((kb > thr) | (eq & admit), 0.0, MASKED)
            bias = jnp.concatenate([bias_s[...]] * (w // tq), axis=1)
            s_all = _dot(k_ref[0, pl.ds(s0, KEY_TILE), :], qbd[...])
            for h in range(n_kv):
                s = s_all[:, h * w:(h + 1) * w] + bias
                m_old = m_refs[h][...]
                m_new = jnp.maximum(m_old, jnp.max(s, axis=0, keepdims=True))
                pe = jnp.exp2(s - m_new).astype(BF16)
                acc_refs[h][...] = jnp.exp2(m_old - m_new) * acc_refs[h][...] + _dot(v_ref[0, h, p], pe)
                m_refs[h][...] = m_new
            return seen_equal

        lax.fori_loop(0, n_tiles, tile, jnp.zeros((1, tq), F32))

    @pl.when(ties_at_threshold)
    def _():
        ri = lax.broadcasted_iota(I32, (KEY_TILE, KEY_TILE), 0)
        ci = lax.broadcasted_iota(I32, (KEY_TILE, KEY_TILE), 1)
        tri[...] = jnp.where(ci < ri, 1.0, 0.0).astype(BF16)
        attend(True)

    @pl.when(jnp.logical_not(ties_at_threshold))
    def _():
        attend(False)

    for h in range(n_kv):
        acc = acc_refs[h][...]
        o = acc[:hd] / acc[hd:hd + 1]
        for g in range(0, group, 2):
            pair = jnp.concatenate([o[:, g * tq:(g + 1) * tq], o[:, (g + 1) * tq:(g + 2) * tq]], axis=0)
            c0 = (h * group + g) * hd
            o_ref[0, :, c0:c0 + 2 * hd] = pair.T.astype(BF16)


def _attention(q, k, v, qi, ki, wi, *, causal, s_lo, s_hi, topk, tq, t_valid, idx_scale):
    b, t, qw = q.shape
    s = k.shape[1]
    n_kv, hd = v.shape[1], v.shape[3] - V_EXTRA
    w = qw // n_kv // hd * tq
    n_tiles_all = s // KEY_TILE
    assert s % KEY_TILE == 0 and t % tq == 0 and topk <= GROUPS and KEY_TILE % GROUPS == 0
    assert tq == 2 * CHUNK == KEY_BLOCK and (qw // n_kv // hd) % 2 == 0 and 2 * hd == 128 and wi.shape[2] == 128
    if causal:
        assert s >= t
    kernel = functools.partial(_attn_kernel, causal=causal, s_lo=s_lo, s_hi=s_hi, topk=float(topk), tq=tq,
                               t_valid=t_valid, n_tiles_all=n_tiles_all, idx_scale=idx_scale)
    whole = lambda shape: pl.BlockSpec(shape, lambda bb, qq: (bb,) + (0,) * (len(shape) - 1),
                                       pipeline_mode=pl.Buffered(1))
    tile = lambda width: pl.BlockSpec((1, tq, width), lambda bb, qq: (bb, qq, 0))
    return pl.pallas_call(
        kernel,
        grid=(b, t // tq),
        in_specs=[tile(qw), whole((1, s, k.shape[2])), whole((1,) + v.shape[1:]), tile(qi.shape[2]),
                  whole((1, s, ki.shape[2])), tile(128)],
        out_specs=tile(qw),
        out_shape=jax.ShapeDtypeStruct((b, t, qw), BF16),
        scratch_shapes=[pltpu.VMEM((n_tiles_all, KEY_TILE, tq), I32),
                        pltpu.VMEM((KEY_TILE, tq), F32),
                        pltpu.VMEM((n_kv * hd, n_kv * w), BF16),
                        pltpu.VMEM((KEY_TILE, KEY_TILE), BF16),
                        *[pltpu.VMEM((1, w), F32)] * n_kv,
                        *[pltpu.VMEM((v.shape[3], w), F32)] * n_kv],
        compiler_params=_params("parallel", "arbitrary"),
        name="dsa_attention",
    )(q, k, v, qi, ki, wi)


def _oproj_kernel(a_ref, x_ref, w_ref, o_ref):
    o_ref[...] = x_ref[...] + _dot(a_ref[...], w_ref[...])


def _oproj(a, x, w, *, row0):
    n, d = x.shape
    nrows = a.shape[0]
    tm = _stream_tile(row0, nrows)
    rows = pl.BlockSpec((tm, d), lambda i: (row0 // tm + i, 0))
    return pl.pallas_call(
        _oproj_kernel,
        grid=(nrows // tm,),
        in_specs=[pl.BlockSpec((tm, a.shape[1]), lambda i: (i, 0)), rows, _const(w.shape)],
        out_specs=rows,
        out_shape=jax.ShapeDtypeStruct((n, d), F32),
        input_output_aliases={1: 0},
        compiler_params=_params("parallel"),
        name="attn_out",
    )(a, x, w)


def _router_kernel(x_ref, g_ref, wr_ref, br_ref, h_ref, comb_ref, pos_ref):
    tm = x_ref.shape[0]
    n_e = wr_ref.shape[0]
    hf = _rms(x_ref[...], g_ref[...])
    h_ref[...] = hf.astype(BF16)
    lg = lax.dot_general(wr_ref[...], hf, (((1,), (1,)), ((), ())), precision=lax.Precision.HIGHEST,
                         preferred_element_type=F32) + br_ref[...]
    ids = lax.broadcasted_iota(I32, (n_e, tm), 0).astype(F32)
    m1 = jnp.max(lg, axis=0, keepdims=True)
    i1 = jnp.min(jnp.where(lg == m1, ids, float(n_e)), axis=0, keepdims=True)
    first = ids == i1
    lg2 = jnp.where(first, -jnp.inf, lg)
    m2 = jnp.max(lg2, axis=0, keepdims=True)
    i2 = jnp.min(jnp.where(lg2 == m2, ids, float(n_e)), axis=0, keepdims=True)
    second = ids == i2
    e2 = jnp.exp(m2 - m1)
    den = 1.0 + e2
    comb_ref[...] = jnp.where(first, 1.0 / den, jnp.where(second, e2 / den, 0.0))
    sel = jnp.where(first | second, 1.0, 0.0)
    ri = lax.broadcasted_iota(I32, (tm, tm), 0)
    ci = lax.broadcasted_iota(I32, (tm, tm), 1)
    before = jnp.where(ri < ci, 1.0, 0.0).astype(BF16)
    slot = _dot(sel.astype(BF16), before)
    pos_ref[...] = jnp.where(sel > 0.0, slot, -1.0)


def _router(x, g, wr_t, br, tm):
    n, d = x.shape
    n_e = wr_t.shape[0]
    return pl.pallas_call(
        _router_kernel,
        grid=(n // tm,),
        in_specs=[pl.BlockSpec((tm, d), lambda i: (i, 0)), _const((1, d)), _const(wr_t.shape), _const((n_e, 1))],
        out_specs=[pl.BlockSpec((tm, d), lambda i: (i, 0)), pl.BlockSpec((n_e, tm), lambda i: (0, i)),
                   pl.BlockSpec((n_e, tm), lambda i: (0, i))],
        out_shape=[jax.ShapeDtypeStruct((n, d), BF16), jax.ShapeDtypeStruct((n_e, n), F32),
                   jax.ShapeDtypeStruct((n_e, n), F32)],
        compiler_params=_params("parallel"),
        name="moe_router",
    )(x, g, wr_t, br)


def _moe_kernel(cnt_ref, h_ref, x_ref, post_ref, pos_ref, comb_ref, wg_ref, wu_ref, wd_ref, gf_ref, o_ref,
                xg, yg, *, n_e, fc):
    tm = h_ref.shape[0]
    e, i = pl.program_id(0), pl.program_id(1)
    n_sub = (cnt_ref[i * n_e + e] + (MOE_ROWS - 1)) // MOE_ROWS
    dff = wg_ref.shape[2]

    def rows(r):
        return pl.ds(pl.multiple_of(r * MOE_ROWS, MOE_ROWS), MOE_ROWS)

    slot_of_token = post_ref[0]

    def gather(r, carry):
        want = lax.broadcasted_iota(I32, (MOE_ROWS, tm), 0) + r * MOE_ROWS
        pick = jnp.where(slot_of_token == want, 1.0, 0.0).astype(BF16)
        xg[rows(r), :] = _dot(pick, h_ref[...]).astype(BF16)
        return carry

    lax.fori_loop(0, n_sub, gather, 0)

    def expert(r, carry):
        xr = xg[rows(r), :]
        y = None
        for c in range(dff // fc):
            a = _dot(xr, wg_ref[0, :, c * fc:(c + 1) * fc])
            u = _dot(xr, wu_ref[0, :, c * fc:(c + 1) * fc])
            part = _dot((a * jax.nn.sigmoid(a) * u).astype(BF16), wd_ref[0, c * fc:(c + 1) * fc, :])
            y = part if y is None else y + part
        yg[rows(r), :] = y.astype(BF16)
        return carry

    lax.fori_loop(0, n_sub, expert, 0)

    onehot = jnp.where(lax.broadcasted_iota(I32, (1, n_e), 1) == e, 1.0, 0.0)
    slot_col = jnp.sum(pos_ref[...] * onehot, axis=1, keepdims=True)
    gate_col = jnp.sum(comb_ref[...] * onehot, axis=1, keepdims=True)
    o_ref[...] = x_ref[...]

    def scatter(r, carry):
        want = (lax.broadcasted_iota(I32, (tm, MOE_ROWS), 1) + r * MOE_ROWS).astype(F32)
        place = jnp.where(slot_col == want, 1.0, 0.0).astype(BF16)
        o_ref[...] += gate_col * _dot(place, yg[rows(r), :])
        return carry

    lax.fori_loop(0, n_sub, scatter, 0)

    @pl.when(e == n_e - 1)
    def _():
        o_ref[...] = _rms(o_ref[...], gf_ref[...])


def _moe(counts, h, x, pos_t, pos, comb, wg, wu, wd, gf, tm):
    n, d = x.shape
    n_e, _, dff = wg.shape
    slots = -(-tm // MOE_ROWS) * MOE_ROWS
    tile = lambda width: pl.BlockSpec((tm, width), lambda e, i, cnt: (i, 0))
    weight = lambda shape: pl.BlockSpec((1,) + shape, lambda e, i, cnt: (e, 0, 0), pipeline_mode=pl.Buffered(1))
    grid_spec = pltpu.PrefetchScalarGridSpec(
        num_scalar_prefetch=1,
        grid=(n_e, n // tm),
        in_specs=[tile(d), tile(d),
                  pl.BlockSpec((1, 1, tm), lambda e, i, cnt: (e, 0, i)),
                  tile(n_e), tile(n_e),
                  weight((d, dff)), weight((d, dff)), weight((dff, d)),
                  pl.BlockSpec((1, d), lambda e, i, cnt: (0, 0))],
        out_specs=tile(d),
        scratch_shapes=[pltpu.VMEM((slots, d), BF16), pltpu.VMEM((slots, d), BF16)],
    )
    return pl.pallas_call(
        functools.partial(_moe_kernel, n_e=n_e, fc=_ff_chunk(dff)),
        grid_spec=grid_spec,
        out_shape=jax.ShapeDtypeStruct((n, d), F32),
        input_output_aliases={2: 0},
        compiler_params=_params("arbitrary", "arbitrary", vmem=60 * 1024 * 1024),
        name="moe_experts",
    )(counts, h, x, pos_t, pos, comb, wg, wu, wd, gf)


def kernel(x_prompt, x_sample, cache_conv, cache_k, cache_v, cache_idx_k, meta_tokens, norm_mix_g, norm_ffn_g, conv_w_pw1, conv_b_pw1, conv_dw_w, conv_dw_b, conv_ln_g, conv_ln_b, conv_w_pw2, attn_w_in, attn_w_out, ffn_w_gate, ffn_w_up, ffn_w_down, moe_w_router, moe_b_router, moe_w_gate, moe_w_up, moe_w_down, final_norm_g):
    bp, seq, d = x_prompt.shape
    bs, ts, _ = x_sample.shape
    past = cache_k.shape[2]
    n_kv, hd = cache_k.shape[3], cache_k.shape[4]
    di = cache_idx_k.shape[3]
    n_meta = meta_tokens.shape[0]
    width = conv_dw_w.shape[1]
    n_e = moe_w_router.shape[2]
    q_width = attn_w_out.shape[1]
    kv_width = n_kv * hd
    n_idx = (attn_w_in.shape[2] - q_width - 2 * kv_width - di) // (di + 1)
    assert width - 1 <= HALO and ts >= HALO and bs % STRIPS == 0 and seq % PROMPT_LEAD == 0 and n_meta <= CHUNK
    assert norm_mix_g.shape[0] == 2, "one conv layer followed by one attention layer"

    lead = PROMPT_LEAD - n_meta
    tp = PROMPT_LEAD + seq
    n_p, n_s = bp * tp, bs * ts
    n = n_p + n_s
    row = lambda a: a.reshape(1, -1)
    bf = lambda a: a.astype(BF16)

    head = jnp.concatenate([jnp.zeros((lead, d), F32), meta_tokens.astype(F32)], axis=0)
    x0 = jnp.concatenate([piece for i in range(bp) for piece in (head, x_prompt[i])] + [x_sample.reshape(n_s, d)],
                         axis=0)

    glu = _glu(x0, row(norm_mix_g[0]), bf(conv_w_pw1[0]), row(conv_b_pw1[0]))
    conv_args = (conv_dw_w[0], row(conv_dw_b[0]), row(conv_ln_g[0]), row(conv_ln_b[0]), bf(conv_w_pw2[0]))
    pad_state = lambda st: jnp.pad(st, ((0, 0), (HALO - (width - 1), 0), (0, 0)))
    strip_p = next(t for t in (96, 64, 32) if tp % (STRIPS * t) == 0)
    x1 = _conv(glu, x0, pad_state(jnp.zeros((bp, width - 1, d), F32)), *conv_args, row0=0, groups=bp,
               steps=tp // (STRIPS * strip_p), strip=strip_p, n_zero=lead, chained=True)
    x1 = _conv(glu, x1, pad_state(cache_conv[0].astype(F32)), *conv_args, row0=n_p, groups=bs // STRIPS,
               steps=1, strip=ts, n_zero=0, chained=False)
    conv_state_p = jnp.stack([glu[(i + 1) * tp - (width - 1):(i + 1) * tp] for i in range(bp)])
    conv_state_s = glu[n_p:].reshape(bs, ts, d)[:, ts - (width - 1):]
    x2 = _ffn(x1, row(norm_ffn_g[0]), bf(ffn_w_gate[0]), bf(ffn_w_up[0]), bf(ffn_w_down[0]))

    w_in = attn_w_in[0]
    c_qi = q_width + 2 * kv_width + n_idx * di
    lane_pad = lambda a: jnp.pad(a, ((0, 0), (0, -a.shape[1] % 128)))
    w_cat = jnp.concatenate([w_in[:, :c_qi], lane_pad(w_in[:, c_qi:c_qi + di]), lane_pad(w_in[:, c_qi + di:])], axis=1)
    wi_off = -(-di // 128) * 128
    assert w_cat.shape[1] - c_qi == wi_off + 128
    tq = 2 * CHUNK

    def stream(row0, b, t, cache, **kw_):
        q, kv, qi, kw = _proj(x2, row(norm_mix_g[1]), bf(w_cat), row0=row0, nrows=b * t, qw=q_width,
                              kvw=2 * kv_width, qiw=n_idx * di, q_scale=hd ** -0.5 * LOG2_E, qi_scale=di ** -0.5)
        new = [a.reshape(b, t, -1) for a in (kv[:, :kv_width], kv[:, kv_width:], kw[:, :di])]
        keys = [a if c is None else jnp.concatenate([c.reshape(b, -1, a.shape[2]).astype(F32), a], axis=1)
                for c, a in zip(cache, new)]
        k_all, v_all, ki_all = [bf(jnp.pad(a, ((0, 0), (0, -a.shape[1] % KEY_TILE), (0, 0)))) for a in keys]
        s = k_all.shape[1]
        v_t = v_all.reshape(b, s // KEY_TILE, KEY_TILE, n_kv, hd).transpose(0, 3, 1, 4, 2)
        ones_row = jnp.zeros((b, n_kv, s // KEY_TILE, V_EXTRA, KEY_TILE), BF16).at[:, :, :, 0, :].set(1.0)
        pad_t = lambda a: jnp.pad(a.reshape(b, t, -1), ((0, 0), (0, -t % tq), (0, 0)))
        o = _attention(pad_t(q), k_all, jnp.concatenate([v_t, ones_row], axis=3), pad_t(qi), ki_all,
                       pad_t(kw[:, wi_off:]), tq=tq, t_valid=t, idx_scale=n_idx ** -0.5, s_hi=keys[0].shape[1], **kw_)
        return o[:, :t].reshape(b * t, q_width), new

    o_p, (k_p, v_p, ki_p) = stream(0, bp, tp, (None, None, None), causal=True, s_lo=lead, topk=min(256, seq // 4))
    o_s, (k_s, v_s, ki_s) = stream(n_p, bs, ts, (cache_k[0], cache_v[0], cache_idx_k[0]), causal=False, s_lo=0,
                                   topk=min(256, (past + ts) // 4))
    x3 = _oproj(o_p, x2, bf(attn_w_out[0]), row0=0)
    x3 = _oproj(o_s, x3, bf(attn_w_out[0]), row0=n_p)

    tm = next((t for t in MOE_TILES if n % t == 0), None) or _row_tile(n)
    h, comb_t, pos_t = _router(x3, row(norm_ffn_g[1]), moe_w_router[0].T, moe_b_router[0].reshape(n_e, 1), tm)
    counts = jnp.sum((pos_t >= 0.0).reshape(n_e, n // tm, tm), axis=2, dtype=I32).T.reshape(-1)
    y = _moe(counts, h, x3, pos_t.astype(I32).reshape(n_e, 1, n), pos_t.T, comb_t.T,
             bf(moe_w_gate[0]), bf(moe_w_up[0]), bf(moe_w_down[0]), row(final_norm_g), tm)

    y_prompt = jnp.stack([y[i * tp + PROMPT_LEAD:(i + 1) * tp] for i in range(bp)])
    y_sample = y[n_p:].reshape(bs, ts, d)
    unlead = lambda a, *tail: a[:, lead:].reshape((1, bp, n_meta + seq) + tail)
    return (y_prompt, y_sample, conv_state_p[None], conv_state_s[None],
            unlead(k_p, n_kv, hd), unlead(v_p, n_kv, hd), unlead(ki_p, di),
            k_s.reshape(1, bs, ts, n_kv, hd), v_s.reshape(1, bs, ts, n_kv, hd), ki_s.reshape(1, bs, ts, di))
```

```python
import functools

import jax
import jax.numpy as jnp
from jax import lax
from jax.experimental import pallas as pl
from jax.experimental.pallas import tpu as pltpu

F32 = jnp.float32
BF16 = jnp.bfloat16
I32 = jnp.int32
I16 = jnp.int16

RMS_EPS = 1e-6
LN_EPS = 1e-5
CHUNK = 64
KEY_BLOCK = 128
KEY_TILE = 512
BITS_PER_CHECK = 4
V_EXTRA = 16
HALF_ROWS = 32
GROUPS = 256
PROMPT_LEAD = 256
MOE_ROWS = 256
MOE_TILES = (896, 1024, 768, 512, 256)
NEG_KEY = -0x7F800000
LOG2_E = 1.4426950408889634
MASKED = -1e30
VMEM_LIMIT = 56 * 1024 * 1024


def _params(*sem, vmem=VMEM_LIMIT):
    return pltpu.CompilerParams(dimension_semantics=sem, vmem_limit_bytes=vmem)


def _const(shape):
    nd = len(shape)
    return pl.BlockSpec(shape, lambda *_: (0,) * nd, pipeline_mode=pl.Buffered(1))


def _rms(x, g):
    ms = jnp.mean(x * x, axis=-1, keepdims=True)
    return x * lax.rsqrt(ms + RMS_EPS) * g


def _dot(a, b):
    return jnp.dot(a, b, preferred_element_type=F32)


def _row_tile(n):
    for t in (1024, 512, 256, 128, 64):
        if n % t == 0:
            return t
    raise ValueError(f"row count {n} has no supported tile")


def _glu_kernel(x_ref, g_ref, w_ref, b_ref, o_ref):
    d = o_ref.shape[-1]
    h = _rms(x_ref[...], g_ref[...]).astype(BF16)
    ag = _dot(h, w_ref[...]) + b_ref[...]
    o_ref[...] = ag[:, :d] * jax.nn.sigmoid(ag[:, d:])


def _glu(x, g, w, b):
    n, d = x.shape
    tm = _row_tile(n)
    return pl.pallas_call(
        _glu_kernel,
        grid=(n // tm,),
        in_specs=[pl.BlockSpec((tm, d), lambda i: (i, 0)), _const((1, d)), _const(w.shape), _const((1, 2 * d))],
        out_specs=pl.BlockSpec((tm, d), lambda i: (i, 0)),
        out_shape=jax.ShapeDtypeStruct((n, d), F32),
        compiler_params=_params("parallel"),
        name="glu",
    )(x, g, w, b)


CONV_ROWS = 32
HALO = 32
STRIPS = 8


def _conv_kernel(g_ref, x_ref, st_ref, dww_ref, dwb_ref, lng_ref, lnb_ref, w2_ref, o_ref, ext, cbuf, hbuf, carry,
                 *, strip, width, n_zero, chained):
    t = pl.program_id(1)
    d = o_ref.shape[-1]
    lead = HALO - (width - 1)
    nl = d // 128

    def put(s, first, count, rows):
        for lb in range(nl):
            ext[lb, pl.ds((HALO + first) * STRIPS + s, count, stride=STRIPS), :] = rows[:, lb * 128:(lb + 1) * 128]

    if chained:
        @pl.when(t == 0)
        def _():
            carry[...] = st_ref[0]
    for s in range(STRIPS):
        put(s, 0, strip, g_ref[s * strip:(s + 1) * strip, :])
        if not chained:
            put(s, -HALO, HALO, st_ref[s])
        elif s == 0:
            put(s, -HALO, HALO, carry[...])
        else:
            put(s, -HALO, HALO, g_ref[s * strip - HALO:s * strip, :])
    if chained:
        carry[...] = g_ref[STRIPS * strip - HALO:STRIPS * strip, :]
    if n_zero:
        @pl.when(t == 0)
        def _():
            for s in range(STRIPS):
                for first, last in ((-HALO, 0), (0, strip)):
                    count = min(s * strip + last, n_zero) - (s * strip + first)
                    if s * strip + first >= 0 and count > 0:
                        put(s, first, count, jnp.zeros((count, d), F32))

    def chunk(i, c):
        r0 = pl.multiple_of(i * CONV_ROWS, CONV_ROWS)
        for lb in range(nl):
            lanes = slice(lb * 128, (lb + 1) * 128)
            taps = [ext[lb, pl.ds(r0 + (lead + k) * STRIPS, CONV_ROWS), :] * dww_ref[k:k + 1, lanes]
                    for k in range(width)]
            while len(taps) > 1:
                taps = [a + b for a, b in zip(taps[::2], taps[1::2])] + ([taps[-1]] if len(taps) % 2 else [])
            cbuf[lb, pl.ds(r0, CONV_ROWS), :] = taps[0] + dwb_ref[:, lanes]
        return c

    lax.fori_loop(0, strip * STRIPS // CONV_ROWS, chunk, 0)
    for s in range(STRIPS):
        acc = jnp.concatenate([cbuf[lb, pl.ds(s, strip, stride=STRIPS), :] for lb in range(nl)], axis=1)
        mu = jnp.mean(acc, axis=-1, keepdims=True)
        xc = acc - mu
        var = jnp.mean(xc * xc, axis=-1, keepdims=True)
        y = xc * lax.rsqrt(var + LN_EPS) * lng_ref[...] + lnb_ref[...]
        hbuf[s * strip:(s + 1) * strip, :] = (y * jax.nn.sigmoid(y)).astype(BF16)
    o_ref[...] = x_ref[...] + _dot(hbuf[...], w2_ref[...])


def _conv(g, x, state, dww, dwb, lng, lnb, w2, *, row0, groups, steps, strip, n_zero, chained):
    n, d = x.shape
    width = dww.shape[0]
    tile = STRIPS * strip
    assert row0 % tile == 0 and strip % CONV_ROWS == 0 and strip >= HALO and (chained or steps == 1)
    blk0 = row0 // tile
    rows = pl.BlockSpec((tile, d), lambda b, t: (blk0 + b * steps + t, 0))
    st_rows = 1 if chained else STRIPS
    in_specs = [rows, rows, pl.BlockSpec((st_rows, HALO, d), lambda b, t: (b, 0, 0)),
                _const(dww.shape), _const((1, d)), _const((1, d)), _const((1, d)), _const(w2.shape)]
    return pl.pallas_call(
        functools.partial(_conv_kernel, strip=strip, width=width, n_zero=n_zero, chained=chained),
        grid=(groups, steps),
        in_specs=in_specs,
        out_specs=rows,
        out_shape=jax.ShapeDtypeStruct((n, d), F32),
        scratch_shapes=[pltpu.VMEM((d // 128, (strip + HALO) * STRIPS, 128), F32),
                        pltpu.VMEM((d // 128, tile, 128), F32),
                        pltpu.VMEM((tile, d), BF16), pltpu.VMEM((HALO, d), F32)],
        input_output_aliases={1: 0},
        compiler_params=_params("arbitrary", "arbitrary"),
        name="conv_mixer",
    )(g, x, state, dww, dwb, lng, lnb, w2)


def _ffn_kernel(x_ref, g_ref, wg_ref, wu_ref, wd_ref, o_ref, *, fc):
    x = x_ref[...]
    h = _rms(x, g_ref[...]).astype(BF16)
    acc = x
    for c in range(wg_ref.shape[1] // fc):
        a = _dot(h, wg_ref[:, c * fc:(c + 1) * fc])
        u = _dot(h, wu_ref[:, c * fc:(c + 1) * fc])
        z = (a * jax.nn.sigmoid(a) * u).astype(BF16)
        acc = acc + _dot(z, wd_ref[c * fc:(c + 1) * fc, :])
    o_ref[...] = acc


def _ff_chunk(dff):
    for parts in (2, 1, 4, 11, 22):
        if dff % parts == 0 and (dff // parts) % 128 == 0:
            return dff // parts
    return dff


def _ffn(x, g, wg, wu, wd):
    n, d = x.shape
    tm = min(_row_tile(n), 512)
    return pl.pallas_call(
        functools.partial(_ffn_kernel, fc=_ff_chunk(wg.shape[1])),
        grid=(n // tm,),
        in_specs=[pl.BlockSpec((tm, d), lambda i: (i, 0)), _const((1, d)),
                  _const(wg.shape), _const(wu.shape), _const(wd.shape)],
        out_specs=pl.BlockSpec((tm, d), lambda i: (i, 0)),
        out_shape=jax.ShapeDtypeStruct((n, d), F32),
        compiler_params=_params("parallel"),
        name="ffn",
    )(x, g, wg, wu, wd)


def _proj_kernel(x_ref, g_ref, w_ref, q_ref, kv_ref, qi_ref, kw_ref, *, qw, kvw, qiw, q_scale, qi_scale):
    h = _rms(x_ref[...], g_ref[...]).astype(BF16)
    r = _dot(h, w_ref[...])
    q_ref[...] = (r[:, :qw] * q_scale).astype(BF16)
    kv_ref[...] = r[:, qw:qw + kvw]
    qi_ref[...] = (r[:, qw + kvw:qw + kvw + qiw] * qi_scale).astype(BF16)
    kw_ref[...] = r[:, qw + kvw + qiw:]


def _stream_tile(row0, nrows, cap=1024):
    return next(t for t in (1024, 512, 256, 128, 64) if t <= cap and row0 % t == 0 and nrows % t == 0)


def _proj(x, g, w, *, row0, nrows, qw, kvw, qiw, q_scale, qi_scale):
    d = x.shape[1]
    tm = _stream_tile(row0, nrows, 512)
    kww = w.shape[1] - qw - kvw - qiw
    row = lambda width: pl.BlockSpec((tm, width), lambda i: (i, 0))
    return pl.pallas_call(
        functools.partial(_proj_kernel, qw=qw, kvw=kvw, qiw=qiw, q_scale=q_scale, qi_scale=qi_scale),
        grid=(nrows // tm,),
        in_specs=[pl.BlockSpec((tm, d), lambda i: (row0 // tm + i, 0)), _const((1, d)), _const(w.shape)],
        out_specs=[row(qw), row(kvw), row(qiw), row(kww)],
        out_shape=[jax.ShapeDtypeStruct((nrows, qw), BF16), jax.ShapeDtypeStruct((nrows, kvw), F32),
                   jax.ShapeDtypeStruct((nrows, qiw), BF16), jax.ShapeDtypeStruct((nrows, kww), F32)],
        compiler_params=_params("parallel"),
        name="attn_proj",
    )(x, g, w)


def _attn_kernel(q_ref, k_ref, v_ref, qi_ref, ki_ref, wi_ref, o_ref, keys, khi, klo, bias_s, qbd, tri, *stats,
                 causal, s_lo, s_hi, topk, tq, t_valid, n_tiles_all, idx_scale):
    hd = v_ref.shape[3] - V_EXTRA
    n_kv = v_ref.shape[1]
    group = q_ref.shape[2] // (n_kv * hd)
    w = group * tq
    di = ki_ref.shape[2]
    n_idx = qi_ref.shape[2] // di
    blocks = KEY_TILE // KEY_BLOCK
    qt = pl.program_id(1)
    n_tiles = (qt + blocks) // blocks if causal else n_tiles_all
    m_refs, acc_refs = stats[:n_kv], stats[n_kv:]

    wi_t = wi_ref[0].T
    wi_row = jnp.concatenate([wi_t[h:h + 1, :] for h in range(n_idx)], axis=1)
    qi_all = qi_ref[0].astype(F32).T
    qi_t = jnp.concatenate([qi_all[h * di:(h + 1) * di, :] for h in range(n_idx)], axis=1).astype(BF16)

    def score_tile(p, carry):
        s0 = pl.multiple_of(p * KEY_TILE, KEY_TILE)
        r = jnp.maximum(_dot(ki_ref[0, pl.ds(s0, KEY_TILE), :], qi_t), 0.0) * wi_row
        sc = r[:, :tq]
        for h in range(1, n_idx):
            sc = sc + r[:, h * tq:(h + 1) * tq]
        bits = pltpu.bitcast(sc * idx_scale, I32)
        keys[p] = jnp.where(bits < 0, jnp.int32(-2 ** 31) - bits, bits)
        return carry

    lax.fori_loop(0, n_tiles, score_tile, 0)

    def blank(p, r0, r1, c0, c1):
        keys[p, r0:r1, c0:c1] = jnp.full((r1 - r0, c1 - c0), NEG_KEY, I32)

    for a0, a1 in ((0, s_lo), (s_hi, n_tiles_all * KEY_TILE)):
        for p in range(a0 // KEY_TILE, -(-a1 // KEY_TILE)):
            r0 = max(a0 - p * KEY_TILE, 0)
            r1 = min(a1 - p * KEY_TILE, KEY_TILE)
            if r1 > r0:
                blank(p, r0, r1, 0, tq)
    if causal:
        for r in range(blocks):
            @pl.when(qt % blocks == r)
            def _():
                blank(qt // blocks, r * KEY_BLOCK + CHUNK, (r + 1) * KEY_BLOCK, 0, CHUNK)
                if r < blocks - 1:
                    blank(qt // blocks, (r + 1) * KEY_BLOCK, KEY_TILE, 0, tq)

    def count_ge(cand):
        def body(p, acc):
            hit = jnp.where(keys[p] >= cand, 1.0, 0.0)
            return acc + jnp.sum(hit.reshape(16, KEY_TILE // 16, tq), axis=0)
        acc = lax.fori_loop(0, n_tiles, body, jnp.zeros((KEY_TILE // 16, tq), F32))
        return jnp.sum(acc, axis=0, keepdims=True)

    def count_ge16(ref, cand):
        c16 = jnp.broadcast_to(cand, (HALF_ROWS, tq)).astype(I16)

        def body(p, acc):
            t = ref[p]
            for r in range(KEY_TILE // HALF_ROWS):
                acc = acc + jnp.where(t[r * HALF_ROWS:(r + 1) * HALF_ROWS] >= c16, jnp.int16(1), jnp.int16(0))
            return acc
        acc = lax.fori_loop(0, n_tiles, body, jnp.zeros((HALF_ROWS, tq), I16))
        return jnp.sum(acc.astype(F32), axis=0, keepdims=True)

    live_query = lax.broadcasted_iota(I32, (1, tq), 1) < (t_valid - qt * tq)

    def some(flags):
        return jnp.max(jnp.where(live_query & flags, 1.0, 0.0)) > 0.0

    def split(p, acc):
        kp = keys[p]
        khi[p] = lax.shift_right_arithmetic(kp, 16).astype(I16)
        klo[p] = ((kp & 0xFFFF) - 0x8000).astype(I16)
        return jnp.maximum(acc, jnp.max(kp.reshape(KEY_TILE // GROUPS, GROUPS, tq), axis=0))

    gmax = lax.fori_loop(0, n_tiles, split, jnp.full((GROUPS, tq), -2 ** 31, I32))
    upper = lax.shift_right_arithmetic(jnp.max(gmax, axis=0, keepdims=True), 16)
    lower = lax.shift_right_arithmetic(jnp.min(gmax, axis=0, keepdims=True), 16)
    open_bits = jnp.where(live_query, jnp.minimum(32 - lax.clz(lower ^ upper), 16).astype(F32), 0.0)
    n_bits_f = jnp.max(open_bits, axis=1, keepdims=True)
    n_bits = jnp.broadcast_to(n_bits_f, (1, tq)).astype(I32)
    one = jnp.ones((1, tq), I32)
    unknown = jnp.full((1, tq), 3e38, F32)

    def high_pass(state):
        it, step, lo, cnt = state
        cand = lo + step
        c = count_ge16(khi, cand)
        take = c >= topk
        return it + 1.0, lax.shift_right_logical(step, one), jnp.where(take, cand, lo), jnp.where(take, c, cnt)

    _, _, thr_hi, cnt_hi = lax.while_loop(
        lambda state: state[0] < jnp.max(n_bits_f), high_pass,
        (jnp.float32(0.0), jnp.where(n_bits > 0, lax.shift_left(one, jnp.maximum(n_bits - 1, 0)), 0),
         jnp.where(n_bits >= 16, -0x8000, upper & -lax.shift_left(one, n_bits)), unknown))
    above = count_ge16(khi, thr_hi + 1)
    k_low = topk - above
    thr_hi16 = jnp.broadcast_to(thr_hi, (HALF_ROWS, tq)).astype(I16)

    def mask_low(p, carry):
        for r in range(KEY_TILE // HALF_ROWS):
            rows = slice(r * HALF_ROWS, (r + 1) * HALF_ROWS)
            klo[p, rows, :] = jnp.where(khi[p, rows, :] == thr_hi16, klo[p, rows, :], jnp.int16(-0x8000))
        return carry

    lax.fori_loop(0, n_tiles, mask_low, 0)

    def low_pass(state):
        it, step, lo, cnt = state
        for _ in range(BITS_PER_CHECK):
            cand = lo + step
            c = count_ge16(klo, cand)
            take = (c >= k_low) & (step != 0)
            step, lo, cnt = lax.shift_right_logical(step, one), jnp.where(take, cand, lo), jnp.where(take, c, cnt)
        return it + float(BITS_PER_CHECK), step, lo, cnt

    _, _, thr_lo, cnt = lax.while_loop(
        lambda state: (state[0] < 16.0) & some(state[3] != k_low), low_pass,
        (jnp.float32(0.0), jnp.full((1, tq), 0x8000, I32), jnp.full((1, tq), -0x8000, I32), cnt_hi - above))
    thr = lax.shift_left(thr_hi, 16) + (thr_lo + 0x8000)
    n_equal_ok = jnp.where(thr == NEG_KEY, 0.0, topk - count_ge(thr + 1))
    ties_at_threshold = some((cnt != k_low) & (thr != NEG_KEY))

    qbd[...] = jnp.zeros(qbd.shape, BF16)
    q_all = q_ref[0].astype(F32).T
    for h in range(n_kv):
        for g in range(group):
            r0 = (h * group + g) * hd
            qbd[h * hd:(h + 1) * hd, h * w + g * tq:h * w + (g + 1) * tq] = q_all[r0:r0 + hd, :].astype(BF16)
    for r in m_refs:
        r[...] = jnp.full(r.shape, MASKED, F32)
    for r in acc_refs:
        r[...] = jnp.zeros(r.shape, F32)

    def attend(ranked):
        def tile(p, seen_equal):
            s0 = pl.multiple_of(p * KEY_TILE, KEY_TILE)
            kb = keys[p]
            eq = kb == thr
            if ranked:
                eqf = jnp.where(eq, 1.0, 0.0)
                admit = _dot(tri[...], eqf.astype(BF16)) + seen_equal < n_equal_ok
                seen_equal = seen_equal + jnp.sum(eqf, axis=0, keepdims=True)
            else:
                admit = n_equal_ok > 0.0
            bias_s[...] = jnp.where((kb > thr) | (eq & admit), 0.0, MASKED)
            bias = jnp.concatenate([bias_s[...]] * (w // tq), axis=1)
            s_all = _dot(k_ref[0, pl.ds(s0, KEY_TILE), :], qbd[...])
            for h in range(n_kv):
                s = s_all[:, h * w:(h + 1) * w] + bias
                m_old = m_refs[h][...]
                m_new = jnp.maximum(m_old, jnp.max(s, axis=0, keepdims=True))
                pe = jnp.exp2(s - m_new).astype(BF16)
                acc_refs[h][...] = jnp.exp2(m_old - m_new) * acc_refs[h][...] + _dot(v_ref[0, h, p], pe)
                m_refs[h][...] = m_new
            return seen_equal

        lax.fori_loop(0, n_tiles, tile, jnp.zeros((1, tq), F32))

    @pl.when(ties_at_threshold)
    def _():
        ri = lax.broadcasted_iota(I32, (KEY_TILE, KEY_TILE), 0)
        ci = lax.broadcasted_iota(I32, (KEY_TILE, KEY_TILE), 1)
        tri[...] = jnp.where(ci < ri, 1.0, 0.0).astype(BF16)
        attend(True)

    @pl.when(jnp.logical_not(ties_at_threshold))
    def _():
        attend(False)

    for h in range(n_kv):
        acc = acc_refs[h][...]
        o = acc[:hd] / acc[hd:hd + 1]
        for g in range(0, group, 2):
            pair = jnp.concatenate([o[:, g * tq:(g + 1) * tq], o[:, (g + 1) * tq:(g + 2) * tq]], axis=0)
            c0 = (h * group + g) * hd
            o_ref[0, :, c0:c0 + 2 * hd] = pair.T.astype(BF16)


def _attention(q, k, v, qi, ki, wi, *, causal, s_lo, s_hi, topk, tq, t_valid, idx_scale):
    b, t, qw = q.shape
    s = k.shape[1]
    n_kv, hd = v.shape[1], v.shape[3] - V_EXTRA
    w = qw // n_kv // hd * tq
    n_tiles_all = s // KEY_TILE
    assert s % KEY_TILE == 0 and t % tq == 0 and topk <= GROUPS and KEY_TILE % GROUPS == 0
    assert tq == 2 * CHUNK == KEY_BLOCK and (qw // n_kv // hd) % 2 == 0 and 2 * hd == 128 and wi.shape[2] == 128
    if causal:
        assert s >= t
    kernel = functools.partial(_attn_kernel, causal=causal, s_lo=s_lo, s_hi=s_hi, topk=float(topk), tq=tq,
                               t_valid=t_valid, n_tiles_all=n_tiles_all, idx_scale=idx_scale)
    whole = lambda shape: pl.BlockSpec(shape, lambda bb, qq: (bb,) + (0,) * (len(shape) - 1),
                                       pipeline_mode=pl.Buffered(1))
    tile = lambda width: pl.BlockSpec((1, tq, width), lambda bb, qq: (bb, qq, 0))
    return pl.pallas_call(
        kernel,
        grid=(b, t // tq),
        in_specs=[tile(qw), whole((1, s, k.shape[2])), whole((1,) + v.shape[1:]), tile(qi.shape[2]),
                  whole((1, s, ki.shape[2])), tile(128)],
        out_specs=tile(qw),
        out_shape=jax.ShapeDtypeStruct((b, t, qw), BF16),
        scratch_shapes=[pltpu.VMEM((n_tiles_all, KEY_TILE, tq), I32),
                        pltpu.VMEM((n_tiles_all, KEY_TILE, tq), I16),
                        pltpu.VMEM((n_tiles_all, KEY_TILE, tq), I16),
                        pltpu.VMEM((KEY_TILE, tq), F32),
                        pltpu.VMEM((n_kv * hd, n_kv * w), BF16),
                        pltpu.VMEM((KEY_TILE, KEY_TILE), BF16),
                        *[pltpu.VMEM((1, w), F32)] * n_kv,
                        *[pltpu.VMEM((v.shape[3], w), F32)] * n_kv],
        compiler_params=_params("parallel", "arbitrary"),
        name="dsa_attention",
    )(q, k, v, qi, ki, wi)


def _oproj_kernel(a_ref, x_ref, w_ref, o_ref):
    o_ref[...] = x_ref[...] + _dot(a_ref[...], w_ref[...])


def _oproj(a, x, w, *, row0):
    n, d = x.shape
    nrows = a.shape[0]
    tm = _stream_tile(row0, nrows)
    rows = pl.BlockSpec((tm, d), lambda i: (row0 // tm + i, 0))
    return pl.pallas_call(
        _oproj_kernel,
        grid=(nrows // tm,),
        in_specs=[pl.BlockSpec((tm, a.shape[1]), lambda i: (i, 0)), rows, _const(w.shape)],
        out_specs=rows,
        out_shape=jax.ShapeDtypeStruct((n, d), F32),
        input_output_aliases={1: 0},
        compiler_params=_params("parallel"),
        name="attn_out",
    )(a, x, w)


def _router_kernel(x_ref, g_ref, wr_ref, br_ref, h_ref, comb_ref, pos_ref):
    tm = x_ref.shape[0]
    n_e = wr_ref.shape[0]
    hf = _rms(x_ref[...], g_ref[...])
    h_ref[...] = hf.astype(BF16)
    lg = lax.dot_general(wr_ref[...], hf, (((1,), (1,)), ((), ())), precision=lax.Precision.HIGHEST,
                         preferred_element_type=F32) + br_ref[...]
    ids = lax.broadcasted_iota(I32, (n_e, tm), 0).astype(F32)
    m1 = jnp.max(lg, axis=0, keepdims=True)
    i1 = jnp.min(jnp.where(lg == m1, ids, float(n_e)), axis=0, keepdims=True)
    first = ids == i1
    lg2 = jnp.where(first, -jnp.inf, lg)
    m2 = jnp.max(lg2, axis=0, keepdims=True)
    i2 = jnp.min(jnp.where(lg2 == m2, ids, float(n_e)), axis=0, keepdims=True)
    second = ids == i2
    e2 = jnp.exp(m2 - m1)
    den = 1.0 + e2
    comb_ref[...] = jnp.where(first, 1.0 / den, jnp.where(second, e2 / den, 0.0))
    sel = jnp.where(first | second, 1.0, 0.0)
    ri = lax.broadcasted_iota(I32, (tm, tm), 0)
    ci = lax.broadcasted_iota(I32, (tm, tm), 1)
    before = jnp.where(ri < ci, 1.0, 0.0).astype(BF16)
    slot = _dot(sel.astype(BF16), before)
    pos_ref[...] = jnp.where(sel > 0.0, slot, -1.0)


def _router(x, g, wr_t, br, tm):
    n, d = x.shape
    n_e = wr_t.shape[0]
    return pl.pallas_call(
        _router_kernel,
        grid=(n // tm,),
        in_specs=[pl.BlockSpec((tm, d), lambda i: (i, 0)), _const((1, d)), _const(wr_t.shape), _const((n_e, 1))],
        out_specs=[pl.BlockSpec((tm, d), lambda i: (i, 0)), pl.BlockSpec((n_e, tm), lambda i: (0, i)),
                   pl.BlockSpec((n_e, tm), lambda i: (0, i))],
        out_shape=[jax.ShapeDtypeStruct((n, d), BF16), jax.ShapeDtypeStruct((n_e, n), F32),
                   jax.ShapeDtypeStruct((n_e, n), F32)],
        compiler_params=_params("parallel"),
        name="moe_router",
    )(x, g, wr_t, br)


def _moe_kernel(cnt_ref, h_ref, x_ref, post_ref, pos_ref, comb_ref, wg_ref, wu_ref, wd_ref, gf_ref, o_ref,
                xg, yg, *, n_e, fc):
    tm = h_ref.shape[0]
    e, i = pl.program_id(0), pl.program_id(1)
    n_sub = (cnt_ref[i * n_e + e] + (MOE_ROWS - 1)) // MOE_ROWS
    dff = wg_ref.shape[2]

    def rows(r):
        return pl.ds(pl.multiple_of(r * MOE_ROWS, MOE_ROWS), MOE_ROWS)

    slot_of_token = post_ref[0]

    def gather(r, carry):
        want = lax.broadcasted_iota(I32, (MOE_ROWS, tm), 0) + r * MOE_ROWS
        pick = jnp.where(slot_of_token == want, 1.0, 0.0).astype(BF16)
        xg[rows(r), :] = _dot(pick, h_ref[...]).astype(BF16)
        return carry

    lax.fori_loop(0, n_sub, gather, 0)

    def expert(r, carry):
        xr = xg[rows(r), :]
        y = None
        for c in range(dff // fc):
            a = _dot(xr, wg_ref[0, :, c * fc:(c + 1) * fc])
            u = _dot(xr, wu_ref[0, :, c * fc:(c + 1) * fc])
            part = _dot((a * jax.nn.sigmoid(a) * u).astype(BF16), wd_ref[0, c * fc:(c + 1) * fc, :])
            y = part if y is None else y + part
        yg[rows(r), :] = y.astype(BF16)
        return carry

    lax.fori_loop(0, n_sub, expert, 0)

    onehot = jnp.where(lax.broadcasted_iota(I32, (1, n_e), 1) == e, 1.0, 0.0)
    slot_col = jnp.sum(pos_ref[...] * onehot, axis=1, keepdims=True)
    gate_col = jnp.sum(comb_ref[...] * onehot, axis=1, keepdims=True)
    o_ref[...] = x_ref[...]

    def scatter(r, carry):
        want = (lax.broadcasted_iota(I32, (tm, MOE_ROWS), 1) + r * MOE_ROWS).astype(F32)
        place = jnp.where(slot_col == want, 1.0, 0.0).astype(BF16)
        o_ref[...] += gate_col * _dot(place, yg[rows(r), :])
        return carry

    lax.fori_loop(0, n_sub, scatter, 0)

    @pl.when(e == n_e - 1)
    def _():
        o_ref[...] = _rms(o_ref[...], gf_ref[...])


def _moe(counts, h, x, pos_t, pos, comb, wg, wu, wd, gf, tm):
    n, d = x.shape
    n_e, _, dff = wg.shape
    slots = -(-tm // MOE_ROWS) * MOE_ROWS
    tile = lambda width: pl.BlockSpec((tm, width), lambda e, i, cnt: (i, 0))
    weight = lambda shape: pl.BlockSpec((1,) + shape, lambda e, i, cnt: (e, 0, 0), pipeline_mode=pl.Buffered(1))
    grid_spec = pltpu.PrefetchScalarGridSpec(
        num_scalar_prefetch=1,
        grid=(n_e, n // tm),
        in_specs=[tile(d), tile(d),
                  pl.BlockSpec((1, 1, tm), lambda e, i, cnt: (e, 0, i)),
                  tile(n_e), tile(n_e),
                  weight((d, dff)), weight((d, dff)), weight((dff, d)),
                  pl.BlockSpec((1, d), lambda e, i, cnt: (0, 0))],
        out_specs=tile(d),
        scratch_shapes=[pltpu.VMEM((slots, d), BF16), pltpu.VMEM((slots, d), BF16)],
    )
    return pl.pallas_call(
        functools.partial(_moe_kernel, n_e=n_e, fc=_ff_chunk(dff)),
        grid_spec=grid_spec,
        out_shape=jax.ShapeDtypeStruct((n, d), F32),
        input_output_aliases={2: 0},
        compiler_params=_params("arbitrary", "arbitrary", vmem=60 * 1024 * 1024),
        name="moe_experts",
    )(counts, h, x, pos_t, pos, comb, wg, wu, wd, gf)


def kernel(x_prompt, x_sample, cache_conv, cache_k, cache_v, cache_idx_k, meta_tokens, norm_mix_g, norm_ffn_g, conv_w_pw1, conv_b_pw1, conv_dw_w, conv_dw_b, conv_ln_g, conv_ln_b, conv_w_pw2, attn_w_in, attn_w_out, ffn_w_gate, ffn_w_up, ffn_w_down, moe_w_router, moe_b_router, moe_w_gate, moe_w_up, moe_w_down, final_norm_g):
    bp, seq, d = x_prompt.shape
    bs, ts, _ = x_sample.shape
    past = cache_k.shape[2]
    n_kv, hd = cache_k.shape[3], cache_k.shape[4]
    di = cache_idx_k.shape[3]
    n_meta = meta_tokens.shape[0]
    width = conv_dw_w.shape[1]
    n_e = moe_w_router.shape[2]
    q_width = attn_w_out.shape[1]
    kv_width = n_kv * hd
    n_idx = (attn_w_in.shape[2] - q_width - 2 * kv_width - di) // (di + 1)
    assert width - 1 <= HALO and ts >= HALO and bs % STRIPS == 0 and seq % PROMPT_LEAD == 0 and n_meta <= CHUNK
    assert norm_mix_g.shape[0] == 2, "one conv layer followed by one attention layer"

    lead = PROMPT_LEAD - n_meta
    tp = PROMPT_LEAD + seq
    n_p, n_s = bp * tp, bs * ts
    n = n_p + n_s
    row = lambda a: a.reshape(1, -1)
    bf = lambda a: a.astype(BF16)

    head = jnp.concatenate([jnp.zeros((lead, d), F32), meta_tokens.astype(F32)], axis=0)
    x0 = jnp.concatenate([piece for i in range(bp) for piece in (head, x_prompt[i])] + [x_sample.reshape(n_s, d)],
                         axis=0)

    glu = _glu(x0, row(norm_mix_g[0]), bf(conv_w_pw1[0]), row(conv_b_pw1[0]))
    conv_args = (conv_dw_w[0], row(conv_dw_b[0]), row(conv_ln_g[0]), row(conv_ln_b[0]), bf(conv_w_pw2[0]))
    pad_state = lambda st: jnp.pad(st, ((0, 0), (HALO - (width - 1), 0), (0, 0)))
    strip_p = next(t for t in (96, 64, 32) if tp % (STRIPS * t) == 0)
    x1 = _conv(glu, x0, pad_state(jnp.zeros((bp, width - 1, d), F32)), *conv_args, row0=0, groups=bp,
               steps=tp // (STRIPS * strip_p), strip=strip_p, n_zero=lead, chained=True)
    x1 = _conv(glu, x1, pad_state(cache_conv[0].astype(F32)), *conv_args, row0=n_p, groups=bs // STRIPS,
               steps=1, strip=ts, n_zero=0, chained=False)
    conv_state_p = jnp.stack([glu[(i + 1) * tp - (width - 1):(i + 1) * tp] for i in range(bp)])
    conv_state_s = glu[n_p:].reshape(bs, ts, d)[:, ts - (width - 1):]
    x2 = _ffn(x1, row(norm_ffn_g[0]), bf(ffn_w_gate[0]), bf(ffn_w_up[0]), bf(ffn_w_down[0]))

    w_in = attn_w_in[0]
    c_qi = q_width + 2 * kv_width + n_idx * di
    lane_pad = lambda a: jnp.pad(a, ((0, 0), (0, -a.shape[1] % 128)))
    w_cat = jnp.concatenate([w_in[:, :c_qi], lane_pad(w_in[:, c_qi:c_qi + di]), lane_pad(w_in[:, c_qi + di:])], axis=1)
    wi_off = -(-di // 128) * 128
    assert w_cat.shape[1] - c_qi == wi_off + 128
    tq = 2 * CHUNK

    def stream(row0, b, t, cache, **kw_):
        q, kv, qi, kw = _proj(x2, row(norm_mix_g[1]), bf(w_cat), row0=row0, nrows=b * t, qw=q_width,
                              kvw=2 * kv_width, qiw=n_idx * di, q_scale=hd ** -0.5 * LOG2_E, qi_scale=di ** -0.5)
        new = [a.reshape(b, t, -1) for a in (kv[:, :kv_width], kv[:, kv_width:], kw[:, :di])]
        keys = [a if c is None else jnp.concatenate([c.reshape(b, -1, a.shape[2]).astype(F32), a], axis=1)
                for c, a in zip(cache, new)]
        k_all, v_all, ki_all = [bf(jnp.pad(a, ((0, 0), (0, -a.shape[1] % KEY_TILE), (0, 0)))) for a in keys]
        s = k_all.shape[1]
        v_t = v_all.reshape(b, s // KEY_TILE, KEY_TILE, n_kv, hd).transpose(0, 3, 1, 4, 2)
        ones_row = jnp.zeros((b, n_kv, s // KEY_TILE, V_EXTRA, KEY_TILE), BF16).at[:, :, :, 0, :].set(1.0)
        pad_t = lambda a: jnp.pad(a.reshape(b, t, -1), ((0, 0), (0, -t % tq), (0, 0)))
        o = _attention(pad_t(q), k_all, jnp.concatenate([v_t, ones_row], axis=3), pad_t(qi), ki_all,
                       pad_t(kw[:, wi_off:]), tq=tq, t_valid=t, idx_scale=n_idx ** -0.5, s_hi=keys[0].shape[1], **kw_)
        return o[:, :t].reshape(b * t, q_width), new

    o_p, (k_p, v_p, ki_p) = stream(0, bp, tp, (None, None, None), causal=True, s_lo=lead, topk=min(256, seq // 4))
    o_s, (k_s, v_s, ki_s) = stream(n_p, bs, ts, (cache_k[0], cache_v[0], cache_idx_k[0]), causal=False, s_lo=0,
                                   topk=min(256, (past + ts) // 4))
    x3 = _oproj(o_p, x2, bf(attn_w_out[0]), row0=0)
    x3 = _oproj(o_s, x3, bf(attn_w_out[0]), row0=n_p)

    tm = next((t for t in MOE_TILES if n % t == 0), None) or _row_tile(n)
    h, comb_t, pos_t = _router(x3, row(norm_ffn_g[1]), moe_w_router[0].T, moe_b_router[0].reshape(n_e, 1), tm)
    counts = jnp.sum((pos_t >= 0.0).reshape(n_e, n // tm, tm), axis=2, dtype=I32).T.reshape(-1)
    y = _moe(counts, h, x3, pos_t.astype(I32).reshape(n_e, 1, n), pos_t.T, comb_t.T,
             bf(moe_w_gate[0]), bf(moe_w_up[0]), bf(moe_w_down[0]), row(final_norm_g), tm)

    y_prompt = jnp.stack([y[i * tp + PROMPT_LEAD:(i + 1) * tp] for i in range(bp)])
    y_sample = y[n_p:].reshape(bs, ts, d)
    unlead = lambda a, *tail: a[:, lead:].reshape((1, bp, n_meta + seq) + tail)
    return (y_prompt, y_sample, conv_state_p[None], conv_state_s[None],
            unlead(k_p, n_kv, hd), unlead(v_p, n_kv, hd), unlead(ki_p, di),
            k_s.reshape(1, bs, ts, n_kv, hd), v_s.reshape(1, bs, ts, n_kv, hd), ki_s.reshape(1, bs, ts, di))
```

```python
import functools

import jax
import jax.numpy as jnp
from jax import lax
from jax.experimental import pallas as pl
from jax.experimental.pallas import tpu as pltpu

F32 = jnp.float32
BF16 = jnp.bfloat16
I32 = jnp.int32

RMS_EPS = 1e-6
LN_EPS = 1e-5
CHUNK = 64
KEY_BLOCK = 128
KEY_TILE = 512
BITS_PER_CHECK = 4
V_EXTRA = 16
GROUPS = 256
PROMPT_LEAD = 256
MOE_ROWS = 256
MOE_TILES = (896, 1024, 768, 512, 256)
NEG_KEY = -0x7F800000
LOG2_E = 1.4426950408889634
MASKED = -1e30
VMEM_LIMIT = 56 * 1024 * 1024


def _params(*sem, vmem=VMEM_LIMIT):
    return pltpu.CompilerParams(dimension_semantics=sem, vmem_limit_bytes=vmem)


def _const(shape):
    nd = len(shape)
    return pl.BlockSpec(shape, lambda *_: (0,) * nd, pipeline_mode=pl.Buffered(1))


def _rms(x, g):
    ms = jnp.mean(x * x, axis=-1, keepdims=True)
    return x * lax.rsqrt(ms + RMS_EPS) * g


def _dot(a, b):
    return jnp.dot(a, b, preferred_element_type=F32)


def _row_tile(n):
    for t in (1024, 512, 256, 128, 64):
        if n % t == 0:
            return t
    raise ValueError(f"row count {n} has no supported tile")


def _glu_kernel(x_ref, g_ref, w_ref, b_ref, o_ref):
    d = o_ref.shape[-1]
    h = _rms(x_ref[...], g_ref[...]).astype(BF16)
    ag = _dot(h, w_ref[...]) + b_ref[...]
    o_ref[...] = ag[:, :d] * jax.nn.sigmoid(ag[:, d:])


def _glu(x, g, w, b):
    n, d = x.shape
    tm = _row_tile(n)
    return pl.pallas_call(
        _glu_kernel,
        grid=(n // tm,),
        in_specs=[pl.BlockSpec((tm, d), lambda i: (i, 0)), _const((1, d)), _const(w.shape), _const((1, 2 * d))],
        out_specs=pl.BlockSpec((tm, d), lambda i: (i, 0)),
        out_shape=jax.ShapeDtypeStruct((n, d), F32),
        compiler_params=_params("parallel"),
        name="glu",
    )(x, g, w, b)


CONV_ROWS = 32
HALO = 32
STRIPS = 8


def _conv_kernel(g_ref, x_ref, st_ref, dww_ref, dwb_ref, lng_ref, lnb_ref, w2_ref, o_ref, ext, cbuf, hbuf, carry,
                 *, strip, width, n_zero, chained):
    t = pl.program_id(1)
    d = o_ref.shape[-1]
    lead = HALO - (width - 1)
    nl = d // 128

    def put(s, first, count, rows):
        for lb in range(nl):
            ext[lb, pl.ds((HALO + first) * STRIPS + s, count, stride=STRIPS), :] = rows[:, lb * 128:(lb + 1) * 128]

    if chained:
        @pl.when(t == 0)
        def _():
            carry[...] = st_ref[0]
    for s in range(STRIPS):
        put(s, 0, strip, g_ref[s * strip:(s + 1) * strip, :])
        if not chained:
            put(s, -HALO, HALO, st_ref[s])
        elif s == 0:
            put(s, -HALO, HALO, carry[...])
        else:
            put(s, -HALO, HALO, g_ref[s * strip - HALO:s * strip, :])
    if chained:
        carry[...] = g_ref[STRIPS * strip - HALO:STRIPS * strip, :]
    if n_zero:
        @pl.when(t == 0)
        def _():
            for s in range(STRIPS):
                for first, last in ((-HALO, 0), (0, strip)):
                    count = min(s * strip + last, n_zero) - (s * strip + first)
                    if s * strip + first >= 0 and count > 0:
                        put(s, first, count, jnp.zeros((count, d), F32))

    def chunk(i, c):
        r0 = pl.multiple_of(i * CONV_ROWS, CONV_ROWS)
        for lb in range(nl):
            lanes = slice(lb * 128, (lb + 1) * 128)
            taps = [ext[lb, pl.ds(r0 + (lead + k) * STRIPS, CONV_ROWS), :] * dww_ref[k:k + 1, lanes]
                    for k in range(width)]
            while len(taps) > 1:
                taps = [a + b for a, b in zip(taps[::2], taps[1::2])] + ([taps[-1]] if len(taps) % 2 else [])
            cbuf[lb, pl.ds(r0, CONV_ROWS), :] = taps[0] + dwb_ref[:, lanes]
        return c

    lax.fori_loop(0, strip * STRIPS // CONV_ROWS, chunk, 0)
    for s in range(STRIPS):
        acc = jnp.concatenate([cbuf[lb, pl.ds(s, strip, stride=STRIPS), :] for lb in range(nl)], axis=1)
        mu = jnp.mean(acc, axis=-1, keepdims=True)
        xc = acc - mu
        var = jnp.mean(xc * xc, axis=-1, keepdims=True)
        y = xc * lax.rsqrt(var + LN_EPS) * lng_ref[...] + lnb_ref[...]
        hbuf[s * strip:(s + 1) * strip, :] = (y * jax.nn.sigmoid(y)).astype(BF16)
    o_ref[...] = x_ref[...] + _dot(hbuf[...], w2_ref[...])


def _conv(g, x, state, dww, dwb, lng, lnb, w2, *, row0, groups, steps, strip, n_zero, chained):
    n, d = x.shape
    width = dww.shape[0]
    tile = STRIPS * strip
    assert row0 % tile == 0 and strip % CONV_ROWS == 0 and strip >= HALO and (chained or steps == 1)
    blk0 = row0 // tile
    rows = pl.BlockSpec((tile, d), lambda b, t: (blk0 + b * steps + t, 0))
    st_rows = 1 if chained else STRIPS
    in_specs = [rows, rows, pl.BlockSpec((st_rows, HALO, d), lambda b, t: (b, 0, 0)),
                _const(dww.shape), _const((1, d)), _const((1, d)), _const((1, d)), _const(w2.shape)]
    return pl.pallas_call(
        functools.partial(_conv_kernel, strip=strip, width=width, n_zero=n_zero, chained=chained),
        grid=(groups, steps),
        in_specs=in_specs,
        out_specs=rows,
        out_shape=jax.ShapeDtypeStruct((n, d), F32),
        scratch_shapes=[pltpu.VMEM((d // 128, (strip + HALO) * STRIPS, 128), F32),
                        pltpu.VMEM((d // 128, tile, 128), F32),
                        pltpu.VMEM((tile, d), BF16), pltpu.VMEM((HALO, d), F32)],
        input_output_aliases={1: 0},
        compiler_params=_params("arbitrary", "arbitrary"),
        name="conv_mixer",
    )(g, x, state, dww, dwb, lng, lnb, w2)


def _ffn_kernel(x_ref, g_ref, wg_ref, wu_ref, wd_ref, o_ref, *, fc):
    x = x_ref[...]
    h = _rms(x, g_ref[...]).astype(BF16)
    acc = x
    for c in range(wg_ref.shape[1] // fc):
        a = _dot(h, wg_ref[:, c * fc:(c + 1) * fc])
        u = _dot(h, wu_ref[:, c * fc:(c + 1) * fc])
        z = (a * jax.nn.sigmoid(a) * u).astype(BF16)
        acc = acc + _dot(z, wd_ref[c * fc:(c + 1) * fc, :])
    o_ref[...] = acc


def _ff_chunk(dff):
    for parts in (2, 1, 4, 11, 22):
        if dff % parts == 0 and (dff // parts) % 128 == 0:
            return dff // parts
    return dff


def _ffn(x, g, wg, wu, wd):
    n, d = x.shape
    tm = min(_row_tile(n), 512)
    return pl.pallas_call(
        functools.partial(_ffn_kernel, fc=_ff_chunk(wg.shape[1])),
        grid=(n // tm,),
        in_specs=[pl.BlockSpec((tm, d), lambda i: (i, 0)), _const((1, d)),
                  _const(wg.shape), _const(wu.shape), _const(wd.shape)],
        out_specs=pl.BlockSpec((tm, d), lambda i: (i, 0)),
        out_shape=jax.ShapeDtypeStruct((n, d), F32),
        compiler_params=_params("parallel"),
        name="ffn",
    )(x, g, wg, wu, wd)


def _proj_kernel(x_ref, g_ref, w_ref, q_ref, kv_ref, qi_ref, kw_ref, *, qw, kvw, qiw, q_scale, qi_scale):
    h = _rms(x_ref[...], g_ref[...]).astype(BF16)
    r = _dot(h, w_ref[...])
    q_ref[...] = (r[:, :qw] * q_scale).astype(BF16)
    kv_ref[...] = r[:, qw:qw + kvw]
    qi_ref[...] = (r[:, qw + kvw:qw + kvw + qiw] * qi_scale).astype(BF16)
    kw_ref[...] = r[:, qw + kvw + qiw:]


def _stream_tile(row0, nrows, cap=1024):
    return next(t for t in (1024, 512, 256, 128, 64) if t <= cap and row0 % t == 0 and nrows % t == 0)


def _proj(x, g, w, *, row0, nrows, qw, kvw, qiw, q_scale, qi_scale):
    d = x.shape[1]
    tm = _stream_tile(row0, nrows, 512)
    kww = w.shape[1] - qw - kvw - qiw
    row = lambda width: pl.BlockSpec((tm, width), lambda i: (i, 0))
    return pl.pallas_call(
        functools.partial(_proj_kernel, qw=qw, kvw=kvw, qiw=qiw, q_scale=q_scale, qi_scale=qi_scale),
        grid=(nrows // tm,),
        in_specs=[pl.BlockSpec((tm, d), lambda i: (row0 // tm + i, 0)), _const((1, d)), _const(w.shape)],
        out_specs=[row(qw), row(kvw), row(qiw), row(kww)],
        out_shape=[jax.ShapeDtypeStruct((nrows, qw), BF16), jax.ShapeDtypeStruct((nrows, kvw), F32),
                   jax.ShapeDtypeStruct((nrows, qiw), BF16), jax.ShapeDtypeStruct((nrows, kww), F32)],
        compiler_params=_params("parallel"),
        name="attn_proj",
    )(x, g, w)


def _attn_kernel(q_ref, k_ref, v_ref, qi_ref, ki_ref, wi_ref, o_ref, keys, bias_s, qbd, tri, vt, *stats,
                 causal, s_lo, s_hi, topk, tq, t_valid, n_tiles_all, idx_scale):
    n_kv, hd = vt.shape[0], vt.shape[2] - V_EXTRA
    group = q_ref.shape[2] // (n_kv * hd)
    w = group * tq
    di = ki_ref.shape[2]
    n_idx = qi_ref.shape[2] // di
    blocks = KEY_TILE // KEY_BLOCK
    qt = pl.program_id(1)
    n_tiles = (qt + blocks) // blocks if causal else n_tiles_all
    m_refs, acc_refs = stats[:n_kv], stats[n_kv:]

    @pl.when(qt == 0)
    def _():
        ones_row = jnp.where(lax.broadcasted_iota(I32, (V_EXTRA, KEY_TILE), 0) == 0, 1.0, 0.0).astype(BF16)

        def transpose_tile(p, carry):
            v_t = v_ref[0, pl.ds(pl.multiple_of(p * KEY_TILE, KEY_TILE), KEY_TILE), :].astype(F32).T
            for h in range(n_kv):
                vt[h, p, 0:hd, :] = v_t[h * hd:(h + 1) * hd, :].astype(BF16)
                vt[h, p, hd:hd + V_EXTRA, :] = ones_row
            return carry

        lax.fori_loop(0, n_tiles_all, transpose_tile, 0)

    wi_t = wi_ref[0].T
    wi_row = jnp.concatenate([wi_t[h:h + 1, :] for h in range(n_idx)], axis=1)
    qi_all = qi_ref[0].astype(F32).T
    qi_t = jnp.concatenate([qi_all[h * di:(h + 1) * di, :] for h in range(n_idx)], axis=1).astype(BF16)

    def score_tile(p, carry):
        s0 = pl.multiple_of(p * KEY_TILE, KEY_TILE)
        r = jnp.maximum(_dot(ki_ref[0, pl.ds(s0, KEY_TILE), :], qi_t), 0.0) * wi_row
        sc = r[:, :tq]
        for h in range(1, n_idx):
            sc = sc + r[:, h * tq:(h + 1) * tq]
        bits = pltpu.bitcast(sc * idx_scale, I32)
        keys[p] = jnp.where(bits < 0, jnp.int32(-2 ** 31) - bits, bits)
        return carry

    lax.fori_loop(0, n_tiles, score_tile, 0)

    def blank(p, r0, r1, c0, c1):
        keys[p, r0:r1, c0:c1] = jnp.full((r1 - r0, c1 - c0), NEG_KEY, I32)

    for a0, a1 in ((0, s_lo), (s_hi, n_tiles_all * KEY_TILE)):
        for p in range(a0 // KEY_TILE, -(-a1 // KEY_TILE)):
            r0 = max(a0 - p * KEY_TILE, 0)
            r1 = min(a1 - p * KEY_TILE, KEY_TILE)
            if r1 > r0:
                blank(p, r0, r1, 0, tq)
    if causal:
        for r in range(blocks):
            @pl.when(qt % blocks == r)
            def _():
                blank(qt // blocks, r * KEY_BLOCK + CHUNK, (r + 1) * KEY_BLOCK, 0, CHUNK)
                if r < blocks - 1:
                    blank(qt // blocks, (r + 1) * KEY_BLOCK, KEY_TILE, 0, tq)

    fold = 16

    def count_ge(cand):
        def body(p, acc):
            hit = jnp.where(keys[p] >= cand, 1.0, 0.0)
            return acc + jnp.sum(hit.reshape(fold, KEY_TILE // fold, tq), axis=0)
        acc = lax.fori_loop(0, n_tiles, body, jnp.zeros((KEY_TILE // fold, tq), F32))
        return jnp.sum(acc, axis=0, keepdims=True)

    live_query = lax.broadcasted_iota(I32, (1, tq), 1) < (t_valid - qt * tq)

    def some(flags):
        return jnp.max(jnp.where(live_query & flags, 1.0, 0.0)) > 0.0

    def group_max(p, acc):
        return jnp.maximum(acc, jnp.max(keys[p].reshape(KEY_TILE // GROUPS, GROUPS, tq), axis=0))

    gmax = lax.fori_loop(0, n_tiles, group_max, jnp.full((GROUPS, tq), -2 ** 31, I32))
    upper = jnp.max(gmax, axis=0, keepdims=True)
    lower = jnp.min(gmax, axis=0, keepdims=True)
    open_bits = jnp.where(live_query, (32 - lax.clz(lower ^ upper)).astype(F32), 0.0)
    n_bits_f = jnp.max(open_bits, axis=1, keepdims=True)
    n_bits = jnp.broadcast_to(n_bits_f, (1, tq)).astype(I32)
    one = jnp.ones((1, tq), I32)
    step0 = jnp.where(n_bits > 0, lax.shift_left(one, jnp.maximum(n_bits - 1, 0)), 0)
    lo0 = jnp.where(n_bits >= 32, jnp.int32(-2 ** 31), upper & -lax.shift_left(one, jnp.minimum(n_bits, 31)))

    def unresolved(state):
        it, _, _, cnt = state
        return (it < jnp.max(n_bits_f)) & some(cnt != topk)

    def bit_pass(state):
        it, step, lo, cnt = state
        for _ in range(BITS_PER_CHECK):
            cand = lo + step
            c = count_ge(cand)
            take = (c >= topk) & (step != 0)
            step, lo, cnt = lax.shift_right_logical(step, one), jnp.where(take, cand, lo), jnp.where(take, c, cnt)
        return it + float(BITS_PER_CHECK), step, lo, cnt

    _, _, thr, cnt = lax.while_loop(unresolved, bit_pass, (jnp.float32(0.0), step0, lo0, jnp.full((1, tq), 3e38, F32)))
    n_equal_ok = jnp.where(thr == NEG_KEY, 0.0, topk - count_ge(thr + 1))
    ties_at_threshold = some((cnt != topk) & (thr != NEG_KEY))

    qbd[...] = jnp.zeros(qbd.shape, BF16)
    q_all = q_ref[0].astype(F32).T
    for h in range(n_kv):
        for g in range(group):
            r0 = (h * group + g) * hd
            qbd[h * hd:(h + 1) * hd, h * w + g * tq:h * w + (g + 1) * tq] = q_all[r0:r0 + hd, :].astype(BF16)
    for r in m_refs:
        r[...] = jnp.full(r.shape, MASKED, F32)
    for r in acc_refs:
        r[...] = jnp.zeros(r.shape, F32)

    def attend(ranked):
        def tile(p, seen_equal):
            s0 = pl.multiple_of(p * KEY_TILE, KEY_TILE)
            kb = keys[p]
            eq = kb == thr
            if ranked:
                eqf = jnp.where(eq, 1.0, 0.0)
                admit = _dot(tri[...], eqf.astype(BF16)) + seen_equal < n_equal_ok
                seen_equal = seen_equal + jnp.sum(eqf, axis=0, keepdims=True)
            else:
                admit = n_equal_ok > 0.0
            bias_s[...] = jnp.where((kb > thr) | (eq & admit), 0.0, MASKED)
            bias = jnp.concatenate([bias_s[...]] * (w // tq), axis=1)
            s_all = _dot(k_ref[0, pl.ds(s0, KEY_TILE), :], qbd[...])
            for h in range(n_kv):
                s = s_all[:, h * w:(h + 1) * w] + bias
                m_old = m_refs[h][...]
                m_new = jnp.maximum(m_old, jnp.max(s, axis=0, keepdims=True))
                pe = jnp.exp2(s - m_new).astype(BF16)
                acc_refs[h][...] = jnp.exp2(m_old - m_new) * acc_refs[h][...] + _dot(vt[h, p], pe)
                m_refs[h][...] = m_new
            return seen_equal

        lax.fori_loop(0, n_tiles, tile, jnp.zeros((1, tq), F32))

    @pl.when(ties_at_threshold)
    def _():
        ri = lax.broadcasted_iota(I32, (KEY_TILE, KEY_TILE), 0)
        ci = lax.broadcasted_iota(I32, (KEY_TILE, KEY_TILE), 1)
        tri[...] = jnp.where(ci < ri, 1.0, 0.0).astype(BF16)
        attend(True)

    @pl.when(jnp.logical_not(ties_at_threshold))
    def _():
        attend(False)

    for h in range(n_kv):
        acc = acc_refs[h][...]
        o = acc[:hd] / acc[hd:hd + 1]
        for g in range(0, group, 2):
            pair = jnp.concatenate([o[:, g * tq:(g + 1) * tq], o[:, (g + 1) * tq:(g + 2) * tq]], axis=0)
            c0 = (h * group + g) * hd
            o_ref[0, :, c0:c0 + 2 * hd] = pair.T.astype(BF16)


def _attention(q, k, v, qi, ki, wi, *, n_kv, causal, s_lo, s_hi, topk, tq, t_valid, idx_scale):
    b, t, qw = q.shape
    s = k.shape[1]
    hd = k.shape[2] // n_kv
    w = qw // n_kv // hd * tq
    n_tiles_all = s // KEY_TILE
    assert s % KEY_TILE == 0 and t % tq == 0 and topk <= GROUPS and KEY_TILE % GROUPS == 0
    assert tq == 2 * CHUNK == KEY_BLOCK and (qw // n_kv // hd) % 2 == 0 and 2 * hd == 128 and wi.shape[2] == 128
    if causal:
        assert s >= t
    kernel = functools.partial(_attn_kernel, causal=causal, s_lo=s_lo, s_hi=s_hi, topk=float(topk), tq=tq,
                               t_valid=t_valid, n_tiles_all=n_tiles_all, idx_scale=idx_scale)
    whole = lambda shape: pl.BlockSpec(shape, lambda bb, qq: (bb,) + (0,) * (len(shape) - 1),
                                       pipeline_mode=pl.Buffered(1))
    tile = lambda width: pl.BlockSpec((1, tq, width), lambda bb, qq: (bb, qq, 0))
    return pl.pallas_call(
        kernel,
        grid=(b, t // tq),
        in_specs=[tile(qw), whole((1, s, k.shape[2])), whole((1, s, v.shape[2])), tile(qi.shape[2]),
                  whole((1, s, ki.shape[2])), tile(128)],
        out_specs=tile(qw),
        out_shape=jax.ShapeDtypeStruct((b, t, qw), BF16),
        scratch_shapes=[pltpu.VMEM((n_tiles_all, KEY_TILE, tq), I32),
                        pltpu.VMEM((KEY_TILE, tq), F32),
                        pltpu.VMEM((n_kv * hd, n_kv * w), BF16),
                        pltpu.VMEM((KEY_TILE, KEY_TILE), BF16),
                        pltpu.VMEM((n_kv, n_tiles_all, hd + V_EXTRA, KEY_TILE), BF16),
                        *[pltpu.VMEM((1, w), F32)] * n_kv,
                        *[pltpu.VMEM((hd + V_EXTRA, w), F32)] * n_kv],
        compiler_params=_params("parallel", "arbitrary"),
        name="dsa_attention",
    )(q, k, v, qi, ki, wi)


def _oproj_kernel(a_ref, x_ref, w_ref, o_ref):
    o_ref[...] = x_ref[...] + _dot(a_ref[...], w_ref[...])


def _oproj(a, x, w, *, row0):
    n, d = x.shape
    nrows = a.shape[0]
    tm = _stream_tile(row0, nrows)
    rows = pl.BlockSpec((tm, d), lambda i: (row0 // tm + i, 0))
    return pl.pallas_call(
        _oproj_kernel,
        grid=(nrows // tm,),
        in_specs=[pl.BlockSpec((tm, a.shape[1]), lambda i: (i, 0)), rows, _const(w.shape)],
        out_specs=rows,
        out_shape=jax.ShapeDtypeStruct((n, d), F32),
        input_output_aliases={1: 0},
        compiler_params=_params("parallel"),
        name="attn_out",
    )(a, x, w)


def _router_kernel(x_ref, g_ref, wr_ref, br_ref, h_ref, comb_ref, pos_ref):
    tm = x_ref.shape[0]
    n_e = wr_ref.shape[0]
    hf = _rms(x_ref[...], g_ref[...])
    h_ref[...] = hf.astype(BF16)
    lg = lax.dot_general(wr_ref[...], hf, (((1,), (1,)), ((), ())), precision=lax.Precision.HIGHEST,
                         preferred_element_type=F32) + br_ref[...]
    ids = lax.broadcasted_iota(I32, (n_e, tm), 0).astype(F32)
    m1 = jnp.max(lg, axis=0, keepdims=True)
    i1 = jnp.min(jnp.where(lg == m1, ids, float(n_e)), axis=0, keepdims=True)
    first = ids == i1
    lg2 = jnp.where(first, -jnp.inf, lg)
    m2 = jnp.max(lg2, axis=0, keepdims=True)
    i2 = jnp.min(jnp.where(lg2 == m2, ids, float(n_e)), axis=0, keepdims=True)
    second = ids == i2
    e2 = jnp.exp(m2 - m1)
    den = 1.0 + e2
    comb_ref[...] = jnp.where(first, 1.0 / den, jnp.where(second, e2 / den, 0.0))
    sel = jnp.where(first | second, 1.0, 0.0)
    ri = lax.broadcasted_iota(I32, (tm, tm), 0)
    ci = lax.broadcasted_iota(I32, (tm, tm), 1)
    before = jnp.where(ri < ci, 1.0, 0.0).astype(BF16)
    slot = _dot(sel.astype(BF16), before)
    pos_ref[...] = jnp.where(sel > 0.0, slot, -1.0)


def _router(x, g, wr_t, br, tm):
    n, d = x.shape
    n_e = wr_t.shape[0]
    return pl.pallas_call(
        _router_kernel,
        grid=(n // tm,),
        in_specs=[pl.BlockSpec((tm, d), lambda i: (i, 0)), _const((1, d)), _const(wr_t.shape), _const((n_e, 1))],
        out_specs=[pl.BlockSpec((tm, d), lambda i: (i, 0)), pl.BlockSpec((n_e, tm), lambda i: (0, i)),
                   pl.BlockSpec((n_e, tm), lambda i: (0, i))],
        out_shape=[jax.ShapeDtypeStruct((n, d), BF16), jax.ShapeDtypeStruct((n_e, n), F32),
                   jax.ShapeDtypeStruct((n_e, n), F32)],
        compiler_params=_params("parallel"),
        name="moe_router",
    )(x, g, wr_t, br)


def _moe_kernel(cnt_ref, h_ref, x_ref, post_ref, pos_ref, comb_ref, wg_ref, wu_ref, wd_ref, gf_ref, o_ref,
                xg, yg, *, n_e, fc):
    tm = h_ref.shape[0]
    e, i = pl.program_id(0), pl.program_id(1)
    n_sub = (cnt_ref[i * n_e + e] + (MOE_ROWS - 1)) // MOE_ROWS
    dff = wg_ref.shape[2]

    def rows(r):
        return pl.ds(pl.multiple_of(r * MOE_ROWS, MOE_ROWS), MOE_ROWS)

    slot_of_token = post_ref[0]

    def gather(r, carry):
        want = lax.broadcasted_iota(I32, (MOE_ROWS, tm), 0) + r * MOE_ROWS
        pick = jnp.where(slot_of_token == want, 1.0, 0.0).astype(BF16)
        xg[rows(r), :] = _dot(pick, h_ref[...]).astype(BF16)
        return carry

    lax.fori_loop(0, n_sub, gather, 0)

    def expert(r, carry):
        xr = xg[rows(r), :]
        y = None
        for c in range(dff // fc):
            a = _dot(xr, wg_ref[0, :, c * fc:(c + 1) * fc])
            u = _dot(xr, wu_ref[0, :, c * fc:(c + 1) * fc])
            part = _dot((a * jax.nn.sigmoid(a) * u).astype(BF16), wd_ref[0, c * fc:(c + 1) * fc, :])
            y = part if y is None else y + part
        yg[rows(r), :] = y.astype(BF16)
        return carry

    lax.fori_loop(0, n_sub, expert, 0)

    onehot = jnp.where(lax.broadcasted_iota(I32, (1, n_e), 1) == e, 1.0, 0.0)
    slot_col = jnp.sum(pos_ref[...] * onehot, axis=1, keepdims=True)
    gate_col = jnp.sum(comb_ref[...] * onehot, axis=1, keepdims=True)
    o_ref[...] = x_ref[...]

    def scatter(r, carry):
        want = (lax.broadcasted_iota(I32, (tm, MOE_ROWS), 1) + r * MOE_ROWS).astype(F32)
        place = jnp.where(slot_col == want, 1.0, 0.0).astype(BF16)
        o_ref[...] += gate_col * _dot(place, yg[rows(r), :])
        return carry

    lax.fori_loop(0, n_sub, scatter, 0)

    @pl.when(e == n_e - 1)
    def _():
        o_ref[...] = _rms(o_ref[...], gf_ref[...])


def _moe(counts, h, x, pos_t, pos, comb, wg, wu, wd, gf, tm):
    n, d = x.shape
    n_e, _, dff = wg.shape
    slots = -(-tm // MOE_ROWS) * MOE_ROWS
    tile = lambda width: pl.BlockSpec((tm, width), lambda e, i, cnt: (i, 0))
    weight = lambda shape: pl.BlockSpec((1,) + shape, lambda e, i, cnt: (e, 0, 0), pipeline_mode=pl.Buffered(1))
    grid_spec = pltpu.PrefetchScalarGridSpec(
        num_scalar_prefetch=1,
        grid=(n_e, n // tm),
        in_specs=[tile(d), tile(d),
                  pl.BlockSpec((1, 1, tm), lambda e, i, cnt: (e, 0, i)),
                  tile(n_e), tile(n_e),
                  weight((d, dff)), weight((d, dff)), weight((dff, d)),
                  pl.BlockSpec((1, d), lambda e, i, cnt: (0, 0))],
        out_specs=tile(d),
        scratch_shapes=[pltpu.VMEM((slots, d), BF16), pltpu.VMEM((slots, d), BF16)],
    )
    return pl.pallas_call(
        functools.partial(_moe_kernel, n_e=n_e, fc=_ff_chunk(dff)),
        grid_spec=grid_spec,
        out_shape=jax.ShapeDtypeStruct((n, d), F32),
        input_output_aliases={2: 0},
        compiler_params=_params("arbitrary", "arbitrary", vmem=60 * 1024 * 1024),
        name="moe_experts",
    )(counts, h, x, pos_t, pos, comb, wg, wu, wd, gf)


def kernel(x_prompt, x_sample, cache_conv, cache_k, cache_v, cache_idx_k, meta_tokens, norm_mix_g, norm_ffn_g, conv_w_pw1, conv_b_pw1, conv_dw_w, conv_dw_b, conv_ln_g, conv_ln_b, conv_w_pw2, attn_w_in, attn_w_out, ffn_w_gate, ffn_w_up, ffn_w_down, moe_w_router, moe_b_router, moe_w_gate, moe_w_up, moe_w_down, final_norm_g):
    bp, seq, d = x_prompt.shape
    bs, ts, _ = x_sample.shape
    past = cache_k.shape[2]
    n_kv, hd = cache_k.shape[3], cache_k.shape[4]
    di = cache_idx_k.shape[3]
    n_meta = meta_tokens.shape[0]
    width = conv_dw_w.shape[1]
    n_e = moe_w_router.shape[2]
    q_width = attn_w_out.shape[1]
    kv_width = n_kv * hd
    n_idx = (attn_w_in.shape[2] - q_width - 2 * kv_width - di) // (di + 1)
    assert width - 1 <= HALO and ts >= HALO and bs % STRIPS == 0 and seq % PROMPT_LEAD == 0 and n_meta <= CHUNK
    assert norm_mix_g.shape[0] == 2, "one conv layer followed by one attention layer"

    lead = PROMPT_LEAD - n_meta
    tp = PROMPT_LEAD + seq
    n_p, n_s = bp * tp, bs * ts
    n = n_p + n_s
    row = lambda a: a.reshape(1, -1)
    bf = lambda a: a.astype(BF16)

    head = jnp.concatenate([jnp.zeros((lead, d), F32), meta_tokens.astype(F32)], axis=0)
    x0 = jnp.concatenate([piece for i in range(bp) for piece in (head, x_prompt[i])] + [x_sample.reshape(n_s, d)],
                         axis=0)

    glu = _glu(x0, row(norm_mix_g[0]), bf(conv_w_pw1[0]), row(conv_b_pw1[0]))
    conv_args = (conv_dw_w[0], row(conv_dw_b[0]), row(conv_ln_g[0]), row(conv_ln_b[0]), bf(conv_w_pw2[0]))
    pad_state = lambda st: jnp.pad(st, ((0, 0), (HALO - (width - 1), 0), (0, 0)))
    strip_p = next(t for t in (96, 64, 32) if tp % (STRIPS * t) == 0)
    x1 = _conv(glu, x0, pad_state(jnp.zeros((bp, width - 1, d), F32)), *conv_args, row0=0, groups=bp,
               steps=tp // (STRIPS * strip_p), strip=strip_p, n_zero=lead, chained=True)
    x1 = _conv(glu, x1, pad_state(cache_conv[0].astype(F32)), *conv_args, row0=n_p, groups=bs // STRIPS,
               steps=1, strip=ts, n_zero=0, chained=False)
    conv_state_p = jnp.stack([glu[(i + 1) * tp - (width - 1):(i + 1) * tp] for i in range(bp)])
    conv_state_s = glu[n_p:].reshape(bs, ts, d)[:, ts - (width - 1):]
    x2 = _ffn(x1, row(norm_ffn_g[0]), bf(ffn_w_gate[0]), bf(ffn_w_up[0]), bf(ffn_w_down[0]))

    w_in = attn_w_in[0]
    c_qi = q_width + 2 * kv_width + n_idx * di
    lane_pad = lambda a: jnp.pad(a, ((0, 0), (0, -a.shape[1] % 128)))
    w_cat = jnp.concatenate([w_in[:, :c_qi], lane_pad(w_in[:, c_qi:c_qi + di]), lane_pad(w_in[:, c_qi + di:])], axis=1)
    wi_off = -(-di // 128) * 128
    assert w_cat.shape[1] - c_qi == wi_off + 128
    tq = 2 * CHUNK

    def stream(row0, b, t, cache, **kw_):
        q, kv, qi, kw = _proj(x2, row(norm_mix_g[1]), bf(w_cat), row0=row0, nrows=b * t, qw=q_width,
                              kvw=2 * kv_width, qiw=n_idx * di, q_scale=hd ** -0.5 * LOG2_E, qi_scale=di ** -0.5)
        new = [a.reshape(b, t, -1) for a in (kv[:, :kv_width], kv[:, kv_width:], kw[:, :di])]
        keys = [a if c is None else jnp.concatenate([c.reshape(b, -1, a.shape[2]).astype(F32), a], axis=1)
                for c, a in zip(cache, new)]
        k_all, v_all, ki_all = [bf(jnp.pad(a, ((0, 0), (0, -a.shape[1] % KEY_TILE), (0, 0)))) for a in keys]
        pad_t = lambda a: jnp.pad(a.reshape(b, t, -1), ((0, 0), (0, -t % tq), (0, 0)))
        o = _attention(pad_t(q), k_all, v_all, pad_t(qi), ki_all, pad_t(kw[:, wi_off:]), n_kv=n_kv, tq=tq,
                       t_valid=t, idx_scale=n_idx ** -0.5, s_hi=keys[0].shape[1], **kw_)
        return o[:, :t].reshape(b * t, q_width), new

    o_p, (k_p, v_p, ki_p) = stream(0, bp, tp, (None, None, None), causal=True, s_lo=lead, topk=min(256, seq // 4))
    o_s, (k_s, v_s, ki_s) = stream(n_p, bs, ts, (cache_k[0], cache_v[0], cache_idx_k[0]), causal=False, s_lo=0,
                                   topk=min(256, (past + ts) // 4))
    x3 = _oproj(o_p, x2, bf(attn_w_out[0]), row0=0)
    x3 = _oproj(o_s, x3, bf(attn_w_out[0]), row0=n_p)

    tm = next((t for t in MOE_TILES if n % t == 0), None) or _row_tile(n)
    h, comb_t, pos_t = _router(x3, row(norm_ffn_g[1]), moe_w_router[0].T, moe_b_router[0].reshape(n_e, 1), tm)
    counts = jnp.sum((pos_t >= 0.0).reshape(n_e, n // tm, tm), axis=2, dtype=I32).T.reshape(-1)
    y = _moe(counts, h, x3, pos_t.astype(I32).reshape(n_e, 1, n), pos_t.T, comb_t.T,
             bf(moe_w_gate[0]), bf(moe_w_up[0]), bf(moe_w_down[0]), row(final_norm_g), tm)

    y_prompt = jnp.stack([y[i * tp + PROMPT_LEAD:(i + 1) * tp] for i in range(bp)])
    y_sample = y[n_p:].reshape(bs, ts, d)
    unlead = lambda a, *tail: a[:, lead:].reshape((1, bp, n_meta + seq) + tail)
    return (y_prompt, y_sample, conv_state_p[None], conv_state_s[None],
            unlead(k_p, n_kv, hd), unlead(v_p, n_kv, hd), unlead(ki_p, di),
            k_s.reshape(1, bs, ts, n_kv, hd), v_s.reshape(1, bs, ts, n_kv, hd), ki_s.reshape(1, bs, ts, di))
```

```python
import functools

import jax
import jax.numpy as jnp
from jax import lax
from jax.experimental import pallas as pl
from jax.experimental.pallas import tpu as pltpu

F32 = jnp.float32
BF16 = jnp.bfloat16
I32 = jnp.int32

RMS_EPS = 1e-6
LN_EPS = 1e-5
CHUNK = 64
KEY_BLOCK = 128
KEY_TILE = 512
BITS_PER_CHECK = 4
V_EXTRA = 16
GROUPS = 256
PROMPT_LEAD = 256
MOE_ROWS = 256
MOE_TILES = (896, 1024, 768, 512, 256)
NEG_KEY = -0x7F800000
LOG2_E = 1.4426950408889634
MASKED = -1e30
VMEM_LIMIT = 56 * 1024 * 1024


def _params(*sem, vmem=VMEM_LIMIT):
    return pltpu.CompilerParams(dimension_semantics=sem, vmem_limit_bytes=vmem)


def _const(shape):
    nd = len(shape)
    return pl.BlockSpec(shape, lambda *_: (0,) * nd, pipeline_mode=pl.Buffered(1))


def _rms(x, g):
    ms = jnp.mean(x * x, axis=-1, keepdims=True)
    return x * lax.rsqrt(ms + RMS_EPS) * g


def _dot(a, b):
    return jnp.dot(a, b, preferred_element_type=F32)


def _row_tile(n):
    for t in (1024, 512, 256, 128, 64):
        if n % t == 0:
            return t
    raise ValueError(f"row count {n} has no supported tile")


def _glu_kernel(x_ref, g_ref, w_ref, b_ref, o_ref):
    d = o_ref.shape[-1]
    h = _rms(x_ref[...], g_ref[...]).astype(BF16)
    ag = _dot(h, w_ref[...]) + b_ref[...]
    o_ref[...] = ag[:, :d] * jax.nn.sigmoid(ag[:, d:])


def _glu(x, g, w, b):
    n, d = x.shape
    tm = _row_tile(n)
    return pl.pallas_call(
        _glu_kernel,
        grid=(n // tm,),
        in_specs=[pl.BlockSpec((tm, d), lambda i: (i, 0)), _const((1, d)), _const(w.shape), _const((1, 2 * d))],
        out_specs=pl.BlockSpec((tm, d), lambda i: (i, 0)),
        out_shape=jax.ShapeDtypeStruct((n, d), F32),
        compiler_params=_params("parallel"),
        name="glu",
    )(x, g, w, b)


CONV_ROWS = 32
HALO = 32
STRIPS = 8


def _conv_kernel(g_ref, x_ref, st_ref, dww_ref, dwb_ref, lng_ref, lnb_ref, w2_ref, o_ref, ext, cbuf, hbuf, carry,
                 *, strip, width, n_zero, chained):
    t = pl.program_id(1)
    d = o_ref.shape[-1]
    lead = HALO - (width - 1)
    nl = d // 128

    def put(s, first, count, rows):
        for lb in range(nl):
            ext[lb, pl.ds((HALO + first) * STRIPS + s, count, stride=STRIPS), :] = rows[:, lb * 128:(lb + 1) * 128]

    if chained:
        @pl.when(t == 0)
        def _():
            carry[...] = st_ref[0]
    for s in range(STRIPS):
        put(s, 0, strip, g_ref[s * strip:(s + 1) * strip, :])
        if not chained:
            put(s, -HALO, HALO, st_ref[s])
        elif s == 0:
            put(s, -HALO, HALO, carry[...])
        else:
            put(s, -HALO, HALO, g_ref[s * strip - HALO:s * strip, :])
    if chained:
        carry[...] = g_ref[STRIPS * strip - HALO:STRIPS * strip, :]
    if n_zero:
        @pl.when(t == 0)
        def _():
            for s in range(STRIPS):
                for first, last in ((-HALO, 0), (0, strip)):
                    count = min(s * strip + last, n_zero) - (s * strip + first)
                    if s * strip + first >= 0 and count > 0:
                        put(s, first, count, jnp.zeros((count, d), F32))

    def chunk(i, c):
        r0 = pl.multiple_of(i * CONV_ROWS, CONV_ROWS)
        for lb in range(nl):
            lanes = slice(lb * 128, (lb + 1) * 128)
            taps = [ext[lb, pl.ds(r0 + (lead + k) * STRIPS, CONV_ROWS), :] * dww_ref[k:k + 1, lanes]
                    for k in range(width)]
            while len(taps) > 1:
                taps = [a + b for a, b in zip(taps[::2], taps[1::2])] + ([taps[-1]] if len(taps) % 2 else [])
            cbuf[lb, pl.ds(r0, CONV_ROWS), :] = taps[0] + dwb_ref[:, lanes]
        return c

    lax.fori_loop(0, strip * STRIPS // CONV_ROWS, chunk, 0)
    for s in range(STRIPS):
        acc = jnp.concatenate([cbuf[lb, pl.ds(s, strip, stride=STRIPS), :] for lb in range(nl)], axis=1)
        mu = jnp.mean(acc, axis=-1, keepdims=True)
        xc = acc - mu
        var = jnp.mean(xc * xc, axis=-1, keepdims=True)
        y = xc * lax.rsqrt(var + LN_EPS) * lng_ref[...] + lnb_ref[...]
        hbuf[s * strip:(s + 1) * strip, :] = (y * jax.nn.sigmoid(y)).astype(BF16)
    o_ref[...] = x_ref[...] + _dot(hbuf[...], w2_ref[...])


def _conv(g, x, state, dww, dwb, lng, lnb, w2, *, row0, groups, steps, strip, n_zero, chained):
    n, d = x.shape
    width = dww.shape[0]
    tile = STRIPS * strip
    assert row0 % tile == 0 and strip % CONV_ROWS == 0 and strip >= HALO and (chained or steps == 1)
    blk0 = row0 // tile
    rows = pl.BlockSpec((tile, d), lambda b, t: (blk0 + b * steps + t, 0))
    st_rows = 1 if chained else STRIPS
    in_specs = [rows, rows, pl.BlockSpec((st_rows, HALO, d), lambda b, t: (b, 0, 0)),
                _const(dww.shape), _const((1, d)), _const((1, d)), _const((1, d)), _const(w2.shape)]
    return pl.pallas_call(
        functools.partial(_conv_kernel, strip=strip, width=width, n_zero=n_zero, chained=chained),
        grid=(groups, steps),
        in_specs=in_specs,
        out_specs=rows,
        out_shape=jax.ShapeDtypeStruct((n, d), F32),
        scratch_shapes=[pltpu.VMEM((d // 128, (strip + HALO) * STRIPS, 128), F32),
                        pltpu.VMEM((d // 128, tile, 128), F32),
                        pltpu.VMEM((tile, d), BF16), pltpu.VMEM((HALO, d), F32)],
        input_output_aliases={1: 0},
        compiler_params=_params("arbitrary", "arbitrary"),
        name="conv_mixer",
    )(g, x, state, dww, dwb, lng, lnb, w2)


def _ffn_kernel(x_ref, g_ref, wg_ref, wu_ref, wd_ref, o_ref, *, fc):
    x = x_ref[...]
    h = _rms(x, g_ref[...]).astype(BF16)
    acc = x
    for c in range(wg_ref.shape[1] // fc):
        a = _dot(h, wg_ref[:, c * fc:(c + 1) * fc])
        u = _dot(h, wu_ref[:, c * fc:(c + 1) * fc])
        z = (a * jax.nn.sigmoid(a) * u).astype(BF16)
        acc = acc + _dot(z, wd_ref[c * fc:(c + 1) * fc, :])
    o_ref[...] = acc


def _ff_chunk(dff):
    for parts in (2, 1, 4, 11, 22):
        if dff % parts == 0 and (dff // parts) % 128 == 0:
            return dff // parts
    return dff


def _ffn(x, g, wg, wu, wd):
    n, d = x.shape
    tm = min(_row_tile(n), 512)
    return pl.pallas_call(
        functools.partial(_ffn_kernel, fc=_ff_chunk(wg.shape[1])),
        grid=(n // tm,),
        in_specs=[pl.BlockSpec((tm, d), lambda i: (i, 0)), _const((1, d)),
                  _const(wg.shape), _const(wu.shape), _const(wd.shape)],
        out_specs=pl.BlockSpec((tm, d), lambda i: (i, 0)),
        out_shape=jax.ShapeDtypeStruct((n, d), F32),
        compiler_params=_params("parallel"),
        name="ffn",
    )(x, g, wg, wu, wd)


def _proj_kernel(x_ref, g_ref, w_ref, q_ref, kv_ref, qi_ref, kw_ref, *, qw, kvw, qiw, q_scale, qi_scale):
    h = _rms(x_ref[...], g_ref[...]).astype(BF16)
    r = _dot(h, w_ref[...])
    q_ref[...] = (r[:, :qw] * q_scale).astype(BF16)
    kv_ref[...] = r[:, qw:qw + kvw]
    qi_ref[...] = (r[:, qw + kvw:qw + kvw + qiw] * qi_scale).astype(BF16)
    kw_ref[...] = r[:, qw + kvw + qiw:]


def _stream_tile(row0, nrows, cap=1024):
    return next(t for t in (1024, 512, 256, 128, 64) if t <= cap and row0 % t == 0 and nrows % t == 0)


def _proj(x, g, w, *, row0, nrows, qw, kvw, qiw, q_scale, qi_scale):
    d = x.shape[1]
    tm = _stream_tile(row0, nrows, 512)
    kww = w.shape[1] - qw - kvw - qiw
    row = lambda width: pl.BlockSpec((tm, width), lambda i: (i, 0))
    return pl.pallas_call(
        functools.partial(_proj_kernel, qw=qw, kvw=kvw, qiw=qiw, q_scale=q_scale, qi_scale=qi_scale),
        grid=(nrows // tm,),
        in_specs=[pl.BlockSpec((tm, d), lambda i: (row0 // tm + i, 0)), _const((1, d)), _const(w.shape)],
        out_specs=[row(qw), row(kvw), row(qiw), row(kww)],
        out_shape=[jax.ShapeDtypeStruct((nrows, qw), BF16), jax.ShapeDtypeStruct((nrows, kvw), F32),
                   jax.ShapeDtypeStruct((nrows, qiw), BF16), jax.ShapeDtypeStruct((nrows, kww), F32)],
        compiler_params=_params("parallel"),
        name="attn_proj",
    )(x, g, w)


def _attn_kernel(q_ref, k_ref, v_ref, qi_ref, ki_ref, wi_ref, o_ref, keys, bias_s, qbd, tri, vt, *stats,
                 causal, s_lo, s_hi, topk, tq, t_valid, n_tiles_all, idx_scale):
    n_kv, hd = vt.shape[0], vt.shape[2] - V_EXTRA
    group = q_ref.shape[2] // (n_kv * hd)
    w = group * tq
    di = ki_ref.shape[2]
    n_idx = qi_ref.shape[2] // di
    blocks = KEY_TILE // KEY_BLOCK
    qt = pl.program_id(1)
    n_tiles = (qt + blocks) // blocks if causal else n_tiles_all
    m_refs, acc_refs = stats[:n_kv], stats[n_kv:]

    @pl.when(qt == 0)
    def _():
        ones_row = jnp.where(lax.broadcasted_iota(I32, (V_EXTRA, KEY_TILE), 0) == 0, 1.0, 0.0).astype(BF16)

        def transpose_tile(p, carry):
            v_t = v_ref[0, pl.ds(pl.multiple_of(p * KEY_TILE, KEY_TILE), KEY_TILE), :].astype(F32).T
            for h in range(n_kv):
                vt[h, p, 0:hd, :] = v_t[h * hd:(h + 1) * hd, :].astype(BF16)
                vt[h, p, hd:hd + V_EXTRA, :] = ones_row
            return carry

        lax.fori_loop(0, n_tiles_all, transpose_tile, 0)

    wi_t = wi_ref[0].T
    wi_row = jnp.concatenate([wi_t[h:h + 1, :] for h in range(n_idx)], axis=1)
    qi_all = qi_ref[0].astype(F32).T
    qi_t = jnp.concatenate([qi_all[h * di:(h + 1) * di, :] for h in range(n_idx)], axis=1).astype(BF16)

    def score_tile(p, carry):
        s0 = pl.multiple_of(p * KEY_TILE, KEY_TILE)
        r = jnp.maximum(_dot(ki_ref[0, pl.ds(s0, KEY_TILE), :], qi_t), 0.0) * wi_row
        sc = r[:, :tq]
        for h in range(1, n_idx):
            sc = sc + r[:, h * tq:(h + 1) * tq]
        bits = pltpu.bitcast(sc * idx_scale, I32)
        keys[p] = jnp.where(bits < 0, jnp.int32(-2 ** 31) - bits, bits)
        return carry

    lax.fori_loop(0, n_tiles, score_tile, 0)

    def blank(p, r0, r1, c0, c1):
        keys[p, r0:r1, c0:c1] = jnp.full((r1 - r0, c1 - c0), NEG_KEY, I32)

    for a0, a1 in ((0, s_lo), (s_hi, n_tiles_all * KEY_TILE)):
        for p in range(a0 // KEY_TILE, -(-a1 // KEY_TILE)):
            r0 = max(a0 - p * KEY_TILE, 0)
            r1 = min(a1 - p * KEY_TILE, KEY_TILE)
            if r1 > r0:
                blank(p, r0, r1, 0, tq)
    if causal:
        for r in range(blocks):
            @pl.when(qt % blocks == r)
            def _():
                blank(qt // blocks, r * KEY_BLOCK + CHUNK, (r + 1) * KEY_BLOCK, 0, CHUNK)
                if r < blocks - 1:
                    blank(qt // blocks, (r + 1) * KEY_BLOCK, KEY_TILE, 0, tq)

    fold = 16

    def count_ge(cand):
        def body(p, acc):
            hit = jnp.where(keys[p] >= cand, 1.0, 0.0)
            return acc + jnp.sum(hit.reshape(fold, KEY_TILE // fold, tq), axis=0)
        acc = lax.fori_loop(0, n_tiles, body, jnp.zeros((KEY_TILE // fold, tq), F32))
        return jnp.sum(acc, axis=0, keepdims=True)

    live_query = lax.broadcasted_iota(I32, (1, tq), 1) < (t_valid - qt * tq)

    def some(flags):
        return jnp.max(jnp.where(live_query & flags, 1.0, 0.0)) > 0.0

    def group_max(p, acc):
        return jnp.maximum(acc, jnp.max(keys[p].reshape(KEY_TILE // GROUPS, GROUPS, tq), axis=0))

    gmax = lax.fori_loop(0, n_tiles, group_max, jnp.full((GROUPS, tq), -2 ** 31, I32))
    upper = jnp.max(gmax, axis=0, keepdims=True)
    lower = jnp.min(gmax, axis=0, keepdims=True)
    open_bits = jnp.where(live_query, (32 - lax.clz(lower ^ upper)).astype(F32), 0.0)
    n_bits_f = jnp.max(open_bits, axis=1, keepdims=True)
    n_bits = jnp.broadcast_to(n_bits_f, (1, tq)).astype(I32)
    one = jnp.ones((1, tq), I32)
    step0 = jnp.where(n_bits > 0, lax.shift_left(one, jnp.maximum(n_bits - 1, 0)), 0)
    lo0 = jnp.where(n_bits >= 32, jnp.int32(-2 ** 31), upper & -lax.shift_left(one, jnp.minimum(n_bits, 31)))

    def unresolved(state):
        it, _, _, cnt = state
        return (it < jnp.max(n_bits_f)) & some(cnt != topk)

    def bit_pass(state):
        it, step, lo, cnt = state
        for _ in range(BITS_PER_CHECK):
            cand = lo + step
            c = count_ge(cand)
            take = (c >= topk) & (step != 0)
            step, lo, cnt = lax.shift_right_logical(step, one), jnp.where(take, cand, lo), jnp.where(take, c, cnt)
        return it + float(BITS_PER_CHECK), step, lo, cnt

    _, _, thr, cnt = lax.while_loop(unresolved, bit_pass, (jnp.float32(0.0), step0, lo0, jnp.full((1, tq), 3e38, F32)))
    n_equal_ok = jnp.where(thr == NEG_KEY, 0.0, topk - count_ge(thr + 1))
    ties_at_threshold = some((cnt != topk) & (thr != NEG_KEY))

    qbd[...] = jnp.zeros(qbd.shape, BF16)
    q_all = q_ref[0].astype(F32).T
    for h in range(n_kv):
        for g in range(group):
            r0 = (h * group + g) * hd
            qbd[h * hd:(h + 1) * hd, h * w + g * tq:h * w + (g + 1) * tq] = q_all[r0:r0 + hd, :].astype(BF16)
    for r in m_refs:
        r[...] = jnp.full(r.shape, MASKED, F32)
    for r in acc_refs:
        r[...] = jnp.zeros(r.shape, F32)

    def attend(ranked):
        def tile(p, seen_equal):
            s0 = pl.multiple_of(p * KEY_TILE, KEY_TILE)
            kb = keys[p]
            eq = kb == thr
            if ranked:
                eqf = jnp.where(eq, 1.0, 0.0)
                admit = _dot(tri[...], eqf.astype(BF16)) + seen_equal < n_equal_ok
                seen_equal = seen_equal + jnp.sum(eqf, axis=0, keepdims=True)
            else:
                admit = n_equal_ok > 0.0
            bias_s[...] = jnp.where((kb > thr) | (eq & admit), 0.0, MASKED)
            bias = jnp.concatenate([bias_s[...]] * (w // tq), axis=1)
            s_all = _dot(k_ref[0, pl.ds(s0, KEY_TILE), :], qbd[...])
            for h in range(n_kv):
                s = s_all[:, h * w:(h + 1) * w] + bias
                m_old = m_refs[h][...]
                m_new = jnp.maximum(m_old, jnp.max(s, axis=0, keepdims=True))
                pe = jnp.exp2(s - m_new).astype(BF16)
                acc_refs[h][...] = jnp.exp2(m_old - m_new) * acc_refs[h][...] + _dot(vt[h, p], pe)
                m_refs[h][...] = m_new
            return seen_equal

        lax.fori_loop(0, n_tiles, tile, jnp.zeros((1, tq), F32))

    @pl.when(ties_at_threshold)
    def _():
        ri = lax.broadcasted_iota(I32, (KEY_TILE, KEY_TILE), 0)
        ci = lax.broadcasted_iota(I32, (KEY_TILE, KEY_TILE), 1)
        tri[...] = jnp.where(ci < ri, 1.0, 0.0).astype(BF16)
        attend(True)

    @pl.when(jnp.logical_not(ties_at_threshold))
    def _():
        attend(False)

    for h in range(n_kv):
        acc = acc_refs[h][...]
        o = acc[:hd] / acc[hd:hd + 1]
        for g in range(0, group, 2):
            pair = jnp.concatenate([o[:, g * tq:(g + 1) * tq], o[:, (g + 1) * tq:(g + 2) * tq]], axis=0)
            c0 = (h * group + g) * hd
            o_ref[0, :, c0:c0 + 2 * hd] = pair.T.astype(BF16)


def _attention(q, k, v, qi, ki, wi, *, n_kv, causal, s_lo, s_hi, topk, tq, t_valid, idx_scale):
    b, t, qw = q.shape
    s = k.shape[1]
    hd = k.shape[2] // n_kv
    w = qw // n_kv // hd * tq
    n_tiles_all = s // KEY_TILE
    assert s % KEY_TILE == 0 and t % tq == 0 and topk <= GROUPS and KEY_TILE % GROUPS == 0
    assert tq == 2 * CHUNK == KEY_BLOCK and (qw // n_kv // hd) % 2 == 0 and 2 * hd == 128 and wi.shape[2] == 128
    if causal:
        assert s >= t
    kernel = functools.partial(_attn_kernel, causal=causal, s_lo=s_lo, s_hi=s_hi, topk=float(topk), tq=tq,
                               t_valid=t_valid, n_tiles_all=n_tiles_all, idx_scale=idx_scale)
    whole = lambda shape: pl.BlockSpec(shape, lambda bb, qq: (bb,) + (0,) * (len(shape) - 1),
                                       pipeline_mode=pl.Buffered(1))
    tile = lambda width: pl.BlockSpec((1, tq, width), lambda bb, qq: (bb, qq, 0))
    return pl.pallas_call(
        kernel,
        grid=(b, t // tq),
        in_specs=[tile(qw), whole((1, s, k.shape[2])), whole((1, s, v.shape[2])), tile(qi.shape[2]),
                  whole((1, s, ki.shape[2])), tile(128)],
        out_specs=tile(qw),
        out_shape=jax.ShapeDtypeStruct((b, t, qw), BF16),
        scratch_shapes=[pltpu.VMEM((n_tiles_all, KEY_TILE, tq), I32),
                        pltpu.VMEM((KEY_TILE, tq), F32),
                        pltpu.VMEM((n_kv * hd, n_kv * w), BF16),
                        pltpu.VMEM((KEY_TILE, KEY_TILE), BF16),
                        pltpu.VMEM((n_kv, n_tiles_all, hd + V_EXTRA, KEY_TILE), BF16),
                        *[pltpu.VMEM((1, w), F32)] * n_kv,
                        *[pltpu.VMEM((hd + V_EXTRA, w), F32)] * n_kv],
        compiler_params=_params("parallel", "arbitrary"),
        name="dsa_attention",
    )(q, k, v, qi, ki, wi)


def _oproj_kernel(a_ref, x_ref, w_ref, o_ref):
    o_ref[...] = x_ref[...] + _dot(a_ref[...], w_ref[...])


def _oproj(a, x, w, *, row0):
    n, d = x.shape
    nrows = a.shape[0]
    tm = _stream_tile(row0, nrows)
    rows = pl.BlockSpec((tm, d), lambda i: (row0 // tm + i, 0))
    return pl.pallas_call(
        _oproj_kernel,
        grid=(nrows // tm,),
        in_specs=[pl.BlockSpec((tm, a.shape[1]), lambda i: (i, 0)), rows, _const(w.shape)],
        out_specs=rows,
        out_shape=jax.ShapeDtypeStruct((n, d), F32),
        input_output_aliases={1: 0},
        compiler_params=_params("parallel"),
        name="attn_out",
    )(a, x, w)


def _router_kernel(x_ref, g_ref, wr_ref, br_ref, h_ref, comb_ref, pos_ref):
    tm = x_ref.shape[0]
    n_e = wr_ref.shape[0]
    hf = _rms(x_ref[...], g_ref[...])
    h_ref[...] = hf.astype(BF16)
    lg = lax.dot_general(wr_ref[...], hf, (((1,), (1,)), ((), ())), precision=lax.Precision.HIGHEST,
                         preferred_element_type=F32) + br_ref[...]
    ids = lax.broadcasted_iota(I32, (n_e, tm), 0).astype(F32)
    m1 = jnp.max(lg, axis=0, keepdims=True)
    i1 = jnp.min(jnp.where(lg == m1, ids, float(n_e)), axis=0, keepdims=True)
    first = ids == i1
    lg2 = jnp.where(first, -jnp.inf, lg)
    m2 = jnp.max(lg2, axis=0, keepdims=True)
    i2 = jnp.min(jnp.where(lg2 == m2, ids, float(n_e)), axis=0, keepdims=True)
    second = ids == i2
    e2 = jnp.exp(m2 - m1)
    den = 1.0 + e2
    comb_ref[...] = jnp.where(first, 1.0 / den, jnp.where(second, e2 / den, 0.0))
    sel = jnp.where(first | second, 1.0, 0.0)
    ri = lax.broadcasted_iota(I32, (tm, tm), 0)
    ci = lax.broadcasted_iota(I32, (tm, tm), 1)
    before = jnp.where(ri < ci, 1.0, 0.0).astype(BF16)
    slot = _dot(sel.astype(BF16), before)
    pos_ref[...] = jnp.where(sel > 0.0, slot, -1.0)


def _router(x, g, wr_t, br, tm):
    n, d = x.shape
    n_e = wr_t.shape[0]
    return pl.pallas_call(
        _router_kernel,
        grid=(n // tm,),
        in_specs=[pl.BlockSpec((tm, d), lambda i: (i, 0)), _const((1, d)), _const(wr_t.shape), _const((n_e, 1))],
        out_specs=[pl.BlockSpec((tm, d), lambda i: (i, 0)), pl.BlockSpec((n_e, tm), lambda i: (0, i)),
                   pl.BlockSpec((n_e, tm), lambda i: (0, i))],
        out_shape=[jax.ShapeDtypeStruct((n, d), BF16), jax.ShapeDtypeStruct((n_e, n), F32),
                   jax.ShapeDtypeStruct((n_e, n), F32)],
        compiler_params=_params("parallel"),
        name="moe_router",
    )(x, g, wr_t, br)


def _moe_kernel(cnt_ref, h_ref, x_ref, post_ref, pos_ref, comb_ref, wg_ref, wu_ref, wd_ref, gf_ref, o_ref,
                xg, yg, *, e, n_e, fc, last):
    tm = h_ref.shape[0]
    i = pl.program_id(0)
    n_sub = (cnt_ref[i * n_e + e] + (MOE_ROWS - 1)) // MOE_ROWS
    dff = wg_ref.shape[2]

    def rows(r):
        return pl.ds(pl.multiple_of(r * MOE_ROWS, MOE_ROWS), MOE_ROWS)

    slot_of_token = post_ref[0]

    def gather(r, carry):
        want = lax.broadcasted_iota(I32, (MOE_ROWS, tm), 0) + r * MOE_ROWS
        pick = jnp.where(slot_of_token == want, 1.0, 0.0).astype(BF16)
        xg[rows(r), :] = _dot(pick, h_ref[...]).astype(BF16)
        return carry

    lax.fori_loop(0, n_sub, gather, 0)

    def expert(r, carry):
        xr = xg[rows(r), :]
        y = None
        for c in range(dff // fc):
            a = _dot(xr, wg_ref[0, :, c * fc:(c + 1) * fc])
            u = _dot(xr, wu_ref[0, :, c * fc:(c + 1) * fc])
            part = _dot((a * jax.nn.sigmoid(a) * u).astype(BF16), wd_ref[0, c * fc:(c + 1) * fc, :])
            y = part if y is None else y + part
        yg[rows(r), :] = y.astype(BF16)
        return carry

    lax.fori_loop(0, n_sub, expert, 0)

    onehot = jnp.where(lax.broadcasted_iota(I32, (1, n_e), 1) == e, 1.0, 0.0)
    slot_col = jnp.sum(pos_ref[...] * onehot, axis=1, keepdims=True)
    gate_col = jnp.sum(comb_ref[...] * onehot, axis=1, keepdims=True)
    o_ref[...] = x_ref[...]

    def scatter(r, carry):
        want = (lax.broadcasted_iota(I32, (tm, MOE_ROWS), 1) + r * MOE_ROWS).astype(F32)
        place = jnp.where(slot_col == want, 1.0, 0.0).astype(BF16)
        o_ref[...] += gate_col * _dot(place, yg[rows(r), :])
        return carry

    lax.fori_loop(0, n_sub, scatter, 0)

    if last:
        o_ref[...] = _rms(o_ref[...], gf_ref[...])


def _moe(counts, h, x, pos_t, pos, comb, wg, wu, wd, gf, tm):
    n, d = x.shape
    n_e, _, dff = wg.shape
    slots = -(-tm // MOE_ROWS) * MOE_ROWS
    tile = lambda width: pl.BlockSpec((tm, width), lambda i, cnt: (i, 0))
    for e in range(n_e):
        weight = lambda shape: pl.BlockSpec((1,) + shape, lambda i, cnt, e=e: (e, 0, 0), pipeline_mode=pl.Buffered(1))
        grid_spec = pltpu.PrefetchScalarGridSpec(
            num_scalar_prefetch=1,
            grid=(n // tm,),
            in_specs=[tile(d), tile(d),
                      pl.BlockSpec((1, 1, tm), lambda i, cnt, e=e: (e, 0, i)),
                      tile(n_e), tile(n_e),
                      weight((d, dff)), weight((d, dff)), weight((dff, d)),
                      pl.BlockSpec((1, d), lambda i, cnt: (0, 0))],
            out_specs=tile(d),
            scratch_shapes=[pltpu.VMEM((slots, d), BF16), pltpu.VMEM((slots, d), BF16)],
        )
        x = pl.pallas_call(
            functools.partial(_moe_kernel, e=e, n_e=n_e, fc=_ff_chunk(dff), last=e == n_e - 1),
            grid_spec=grid_spec,
            out_shape=jax.ShapeDtypeStruct((n, d), F32),
            input_output_aliases={2: 0},
            compiler_params=_params("arbitrary", vmem=60 * 1024 * 1024),
            name="moe_expert",
        )(counts, h, x, pos_t, pos, comb, wg, wu, wd, gf)
    return x


def kernel(x_prompt, x_sample, cache_conv, cache_k, cache_v, cache_idx_k, meta_tokens, norm_mix_g, norm_ffn_g, conv_w_pw1, conv_b_pw1, conv_dw_w, conv_dw_b, conv_ln_g, conv_ln_b, conv_w_pw2, attn_w_in, attn_w_out, ffn_w_gate, ffn_w_up, ffn_w_down, moe_w_router, moe_b_router, moe_w_gate, moe_w_up, moe_w_down, final_norm_g):
    bp, seq, d = x_prompt.shape
    bs, ts, _ = x_sample.shape
    past = cache_k.shape[2]
    n_kv, hd = cache_k.shape[3], cache_k.shape[4]
    di = cache_idx_k.shape[3]
    n_meta = meta_tokens.shape[0]
    width = conv_dw_w.shape[1]
    n_e = moe_w_router.shape[2]
    q_width = attn_w_out.shape[1]
    kv_width = n_kv * hd
    n_idx = (attn_w_in.shape[2] - q_width - 2 * kv_width - di) // (di + 1)
    assert width - 1 <= HALO and ts >= HALO and bs % STRIPS == 0 and seq % PROMPT_LEAD == 0 and n_meta <= CHUNK
    assert norm_mix_g.shape[0] == 2, "one conv layer followed by one attention layer"

    lead = PROMPT_LEAD - n_meta
    tp = PROMPT_LEAD + seq
    n_p, n_s = bp * tp, bs * ts
    n = n_p + n_s
    row = lambda a: a.reshape(1, -1)
    bf = lambda a: a.astype(BF16)

    head = jnp.concatenate([jnp.zeros((lead, d), F32), meta_tokens.astype(F32)], axis=0)
    x0 = jnp.concatenate([piece for i in range(bp) for piece in (head, x_prompt[i])] + [x_sample.reshape(n_s, d)],
                         axis=0)

    glu = _glu(x0, row(norm_mix_g[0]), bf(conv_w_pw1[0]), row(conv_b_pw1[0]))
    conv_args = (conv_dw_w[0], row(conv_dw_b[0]), row(conv_ln_g[0]), row(conv_ln_b[0]), bf(conv_w_pw2[0]))
    pad_state = lambda st: jnp.pad(st, ((0, 0), (HALO - (width - 1), 0), (0, 0)))
    strip_p = next(t for t in (96, 64, 32) if tp % (STRIPS * t) == 0)
    x1 = _conv(glu, x0, pad_state(jnp.zeros((bp, width - 1, d), F32)), *conv_args, row0=0, groups=bp,
               steps=tp // (STRIPS * strip_p), strip=strip_p, n_zero=lead, chained=True)
    x1 = _conv(glu, x1, pad_state(cache_conv[0].astype(F32)), *conv_args, row0=n_p, groups=bs // STRIPS,
               steps=1, strip=ts, n_zero=0, chained=False)
    conv_state_p = jnp.stack([glu[(i + 1) * tp - (width - 1):(i + 1) * tp] for i in range(bp)])
    conv_state_s = glu[n_p:].reshape(bs, ts, d)[:, ts - (width - 1):]
    x2 = _ffn(x1, row(norm_ffn_g[0]), bf(ffn_w_gate[0]), bf(ffn_w_up[0]), bf(ffn_w_down[0]))

    w_in = attn_w_in[0]
    c_qi = q_width + 2 * kv_width + n_idx * di
    lane_pad = lambda a: jnp.pad(a, ((0, 0), (0, -a.shape[1] % 128)))
    w_cat = jnp.concatenate([w_in[:, :c_qi], lane_pad(w_in[:, c_qi:c_qi + di]), lane_pad(w_in[:, c_qi + di:])], axis=1)
    wi_off = -(-di // 128) * 128
    assert w_cat.shape[1] - c_qi == wi_off + 128
    tq = 2 * CHUNK

    def stream(row0, b, t, cache, **kw_):
        q, kv, qi, kw = _proj(x2, row(norm_mix_g[1]), bf(w_cat), row0=row0, nrows=b * t, qw=q_width,
                              kvw=2 * kv_width, qiw=n_idx * di, q_scale=hd ** -0.5 * LOG2_E, qi_scale=di ** -0.5)
        new = [a.reshape(b, t, -1) for a in (kv[:, :kv_width], kv[:, kv_width:], kw[:, :di])]
        keys = [a if c is None else jnp.concatenate([c.reshape(b, -1, a.shape[2]).astype(F32), a], axis=1)
                for c, a in zip(cache, new)]
        k_all, v_all, ki_all = [bf(jnp.pad(a, ((0, 0), (0, -a.shape[1] % KEY_TILE), (0, 0)))) for a in keys]
        pad_t = lambda a: jnp.pad(a.reshape(b, t, -1), ((0, 0), (0, -t % tq), (0, 0)))
        o = _attention(pad_t(q), k_all, v_all, pad_t(qi), ki_all, pad_t(kw[:, wi_off:]), n_kv=n_kv, tq=tq,
                       t_valid=t, idx_scale=n_idx ** -0.5, s_hi=keys[0].shape[1], **kw_)
        return o[:, :t].reshape(b * t, q_width), new

    o_p, (k_p, v_p, ki_p) = stream(0, bp, tp, (None, None, None), causal=True, s_lo=lead, topk=min(256, seq // 4))
    o_s, (k_s, v_s, ki_s) = stream(n_p, bs, ts, (cache_k[0], cache_v[0], cache_idx_k[0]), causal=False, s_lo=0,
                                   topk=min(256, (past + ts) // 4))
    x3 = _oproj(o_p, x2, bf(attn_w_out[0]), row0=0)
    x3 = _oproj(o_s, x3, bf(attn_w_out[0]), row0=n_p)

    tm = next((t for t in MOE_TILES if n % t == 0), None) or _row_tile(n)
    h, comb_t, pos_t = _router(x3, row(norm_ffn_g[1]), moe_w_router[0].T, moe_b_router[0].reshape(n_e, 1), tm)
    counts = jnp.sum((pos_t >= 0.0).reshape(n_e, n // tm, tm), axis=2, dtype=I32).T.reshape(-1)
    y = _moe(counts, h, x3, pos_t.astype(I32).reshape(n_e, 1, n), pos_t.T, comb_t.T,
             bf(moe_w_gate[0]), bf(moe_w_up[0]), bf(moe_w_down[0]), row(final_norm_g), tm)

    y_prompt = jnp.stack([y[i * tp + PROMPT_LEAD:(i + 1) * tp] for i in range(bp)])
    y_sample = y[n_p:].reshape(bs, ts, d)
    unlead = lambda a, *tail: a[:, lead:].reshape((1, bp, n_meta + seq) + tail)
    return (y_prompt, y_sample, conv_state_p[None], conv_state_s[None],
            unlead(k_p, n_kv, hd), unlead(v_p, n_kv, hd), unlead(ki_p, di),
            k_s.reshape(1, bs, ts, n_kv, hd), v_s.reshape(1, bs, ts, n_kv, hd), ki_s.reshape(1, bs, ts, di))
```
